```python
import math
import jax, jax.numpy as jnp
from jax import lax
import numpy as np

D_MODEL = 1024
BATCH = 8
SEQ = 4096
DEPTH = 1

PLE_DIM = 256
ATTN_WIDTH = D_MODEL // 2
CONV_WIDTH = D_MODEL - ATTN_WIDTH
ATTN_HEAD_DIM = 64
N_ATTN_HEADS = ATTN_WIDTH // (2 * ATTN_HEAD_DIM)
ATTN_V_DIM = 2 * ATTN_HEAD_DIM
QK_COLS = N_ATTN_HEADS * 2 * ATTN_HEAD_DIM
V_COLS = N_ATTN_HEADS * ATTN_V_DIM
IN_COLS = 2 * QK_COLS + V_COLS + 2 * CONV_WIDTH
CONV_KERNEL = 31
ROPE_THETA = 10000.0
Q_BLOCK = 128
N_PEER_HEADS = 8
N_KEYS = 128
N_EXPERTS = N_KEYS * N_KEYS
PEER_TOPK = 16
PEER_QUERY_DIM = 256
PEER_HALF = PEER_QUERY_DIM // 2
PEER_CHUNK = 128
EPS = 1e-6

kernel_name = "hymba_conv_diffattn_peer_ple"


def rms_norm(x, g):
    xf = x.astype(jnp.float32)
    y = xf * lax.rsqrt(jnp.mean(xf * xf, axis=-1, keepdims=True) + EPS)
    return (y * g.astype(jnp.float32)).astype(x.dtype)


def layer_norm(x, g, b):
    xf = x.astype(jnp.float32)
    mu = jnp.mean(xf, axis=-1, keepdims=True)
    var = jnp.mean(jnp.square(xf - mu), axis=-1, keepdims=True)
    y = (xf - mu) * lax.rsqrt(var + EPS)
    return (y * g.astype(jnp.float32) + b.astype(jnp.float32)).astype(x.dtype)


def rope(x, pos):
    d = x.shape[-1]
    half = d // 2
    inv_freq = 1.0 / (ROPE_THETA ** (jnp.arange(half, dtype=jnp.float32) * 2.0 / d))
    ang = pos.astype(jnp.float32)[:, None] * inv_freq[None, :]
    cos, sin = jnp.cos(ang), jnp.sin(ang)
    x1, x2 = x[..., :half], x[..., half:]
    return jnp.concatenate([x1 * cos - x2 * sin, x2 * cos + x1 * sin], axis=-1)


def diff_attention(q, k, v, lam, subln_g, lambda_init):
    B, H, _, S, d = q.shape
    pos = jnp.arange(S)
    qf = rope(q.astype(jnp.float32), pos) * (d ** -0.5)
    kf = rope(k.astype(jnp.float32), pos)
    vf = v.astype(jnp.float32)
    nb = S // Q_BLOCK
    qb = jnp.moveaxis(qf.reshape(B, H, 2, nb, Q_BLOCK, d), 3, 0)

    def block(args):
        qblk, i = args
        s = jnp.einsum('bhcqd,bhckd->bhcqk', qblk, kf)
        qpos = i * Q_BLOCK + jnp.arange(Q_BLOCK)
        mask = pos[None, :] <= qpos[:, None]
        s = jnp.where(mask, s, -jnp.inf)
        pr = jax.nn.softmax(s, axis=-1)
        a = pr[:, :, 0] - lam * pr[:, :, 1]
        return jnp.einsum('bhqk,bhkd->bhqd', a, vf)

    out = lax.map(block, (qb, jnp.arange(nb)))
    out = out.transpose(1, 0, 3, 2, 4).reshape(B, S, H, -1)
    out = rms_norm(out, subln_g) * (1.0 - lambda_init)
    return out.reshape(B, S, -1)


def conv_module(cv, cg, w, b, ln_g, ln_b):
    u = cv * jax.nn.sigmoid(cg)
    y = lax.conv_general_dilated(
        u, w[:, None, :].astype(u.dtype), window_strides=(1,),
        padding=[(CONV_KERNEL - 1, 0)], dimension_numbers=('NWC', 'WIO', 'NWC'),
        feature_group_count=CONV_WIDTH) + b
    return jax.nn.silu(layer_norm(y, ln_g, ln_b))


def peer(m, wq, keys, u_tab, v_tab):
    B, S, D = m.shape
    T = B * S
    mf = m.reshape(T, D)
    q = (mf @ wq).reshape(T, N_PEER_HEADS, 2, PEER_HALF)
    s = jnp.einsum('thcd,cnd->thcn', q.astype(jnp.float32), keys.astype(jnp.float32))
    sv, si = lax.top_k(s, PEER_TOPK)
    cand = sv[:, :, 0, :, None] + sv[:, :, 1, None, :]
    cs, ci = lax.top_k(cand.reshape(T, N_PEER_HEADS, PEER_TOPK * PEER_TOPK), PEER_TOPK)
    e1 = jnp.take_along_axis(si[:, :, 0], ci // PEER_TOPK, axis=-1)
    e2 = jnp.take_along_axis(si[:, :, 1], ci % PEER_TOPK, axis=-1)
    hk = N_PEER_HEADS * PEER_TOPK
    experts = (e1 * N_KEYS + e2).reshape(T, hk)
    gates = jax.nn.softmax(cs, axis=-1).reshape(T, hk)
    nc = T // PEER_CHUNK

    def chunk(args):
        xc, ec, gc = args
        u = u_tab[ec]
        a = jnp.einsum('cd,ckd->ck', xc, u).astype(jnp.float32)
        h = jax.nn.gelu(a, approximate=False) * gc
        vv = v_tab[ec]
        return jnp.einsum('ck,ckd->cd', h.astype(vv.dtype), vv)

    y = lax.map(chunk, (mf.reshape(nc, PEER_CHUNK, D), experts.reshape(nc, PEER_CHUNK, hk),
                        gates.reshape(nc, PEER_CHUNK, hk)))
    return y.reshape(B, S, D).astype(m.dtype)


def setup_inputs(seed: int = 0) -> dict:
    key = jax.random.key(seed)
    ks = jax.random.split(key, 24)
    f32 = jnp.float32
    nrm = lambda k, shape, s: jax.random.normal(k, shape, f32) * s
    gain = lambda k, shape: 1.0 + 0.02 * jax.random.normal(k, shape, f32)
    L, D = DEPTH, D_MODEL
    return {
        "x": nrm(ks[0], (BATCH, SEQ, D), 1.0),
        "p": nrm(ks[1], (DEPTH, BATCH, SEQ, PLE_DIM), 1.0),
        "attn_norm_g": gain(ks[2], (L, D)),
        "w_in": nrm(ks[3], (L, D, IN_COLS), D ** -0.5),
        "lambda_q1": nrm(ks[4], (L, ATTN_HEAD_DIM), 0.1),
        "lambda_k1": nrm(ks[5], (L, ATTN_HEAD_DIM), 0.1),
        "lambda_q2": nrm(ks[6], (L, ATTN_HEAD_DIM), 0.1),
        "lambda_k2": nrm(ks[7], (L, ATTN_HEAD_DIM), 0.1),
        "subln_g": gain(ks[8], (L, ATTN_V_DIM)),
        "conv_w": nrm(ks[9], (L, CONV_KERNEL, CONV_WIDTH), CONV_KERNEL ** -0.5),
        "conv_b": nrm(ks[10], (L, CONV_WIDTH), 0.02),
        "conv_ln_g": gain(ks[11], (L, CONV_WIDTH)),
        "conv_ln_b": nrm(ks[12], (L, CONV_WIDTH), 0.02),
        "w_out": nrm(ks[13], (L, D, D), D ** -0.5),
        "ffn_norm_g": gain(ks[14], (L, D)),
        "peer_wq": nrm(ks[15], (L, D, N_PEER_HEADS * PEER_QUERY_DIM), D ** -0.5),
        "peer_keys": nrm(ks[16], (L, 2, N_KEYS, PEER_HALF), PEER_HALF ** -0.5),
        "peer_u": nrm(ks[17], (L, N_EXPERTS, D), D ** -0.5),
        "peer_v": nrm(ks[18], (L, N_EXPERTS, D), N_PEER_HEADS ** -0.5),
        "ple_norm_g": gain(ks[19], (L, D)),
        "ple_w_gate": nrm(ks[20], (L, D, D), D ** -0.5),
        "ple_w_proj": nrm(ks[21], (L, PLE_DIM, D), PLE_DIM ** -0.5),
        "final_norm_g": gain(ks[22], (D,)),
    }


def reference(x, p, attn_norm_g, w_in, lambda_q1, lambda_k1, lambda_q2, lambda_k2, subln_g,
              conv_w, conv_b, conv_ln_g, conv_ln_b, w_out, ffn_norm_g, peer_wq, peer_keys,
              peer_u, peer_v, ple_norm_g, ple_w_gate, ple_w_proj, final_norm_g):
    B, S, D = x.shape
    H, d, dv = N_ATTN_HEADS, ATTN_HEAD_DIM, ATTN_V_DIM
    h = x
    for l in range(DEPTH):
        a = rms_norm(h, attn_norm_g[l])
        z = a @ w_in[l]
        q = z[..., :QK_COLS].reshape(B, S, H, 2, d).transpose(0, 2, 3, 1, 4)
        k = z[..., QK_COLS:2 * QK_COLS].reshape(B, S, H, 2, d).transpose(0, 2, 3, 1, 4)
        v = z[..., 2 * QK_COLS:2 * QK_COLS + V_COLS].reshape(B, S, H, dv).transpose(0, 2, 1, 3)
        c0 = 2 * QK_COLS + V_COLS
        cv = z[..., c0:c0 + CONV_WIDTH]
        cg = z[..., c0 + CONV_WIDTH:]
        lambda_init = 0.8 - 0.6 * math.exp(-0.3 * l)
        lam = (jnp.exp(jnp.sum(lambda_q1[l].astype(jnp.float32) * lambda_k1[l].astype(jnp.float32)))
               - jnp.exp(jnp.sum(lambda_q2[l].astype(jnp.float32) * lambda_k2[l].astype(jnp.float32)))
               + lambda_init)
        attn_out = diff_attention(q, k, v, lam, subln_g[l], lambda_init).astype(h.dtype)
        conv_out = conv_module(cv, cg, conv_w[l], conv_b[l], conv_ln_g[l], conv_ln_b[l]).astype(h.dtype)
        h = h + jnp.concatenate([attn_out, conv_out], axis=-1) @ w_out[l]
        h = h + peer(rms_norm(h, ffn_norm_g[l]), peer_wq[l], peer_keys[l], peer_u[l], peer_v[l])
        e = p[l] @ ple_w_proj[l]
        gate = jax.nn.sigmoid(rms_norm(h, ple_norm_g[l]) @ ple_w_gate[l])
        h = h + e * gate
    return rms_norm(h, final_norm_g)
```

```python
import functools
import math

import numpy as np
import jax
import jax.numpy as jnp
from jax import lax
from jax.experimental import pallas as pl
from jax.experimental.pallas import tpu as pltpu

EPS = 1e-6
ROPE_THETA = 10000.0
ATTN_HEAD_DIM = 64
N_ATTN_HEADS = 4
CONV_KERNEL = 31
N_PEER_HEADS = 8
N_KEYS = 128
PEER_TOPK = 16
PEER_HALF = 128

LANES = 128
CONV_HALO = 32
VMEM_LIMIT = 56 * 1024 * 1024

F32 = jnp.float32
BF16 = jnp.bfloat16
NEG_INF = float("-inf")


def _rms(x, g):
    ms = jnp.mean(x * x, axis=-1, keepdims=True)
    return x * lax.rsqrt(ms + EPS) * g


def _params(sem):
    return pltpu.CompilerParams(dimension_semantics=sem, vmem_limit_bytes=VMEM_LIMIT)


def _inproj_kernel(x_ref, g_ref, w_ref, cos_ref, sin_ref, qk_ref, v_ref, c_ref, *, qk_cols, v_cols):
    a = _rms(x_ref[...], g_ref[...]).astype(BF16)
    cos = cos_ref[...]
    sin = sin_ref[...]
    for col in range(0, qk_cols, 2 * LANES):
        z = jnp.dot(a, w_ref[:, col:col + 2 * LANES], preferred_element_type=F32)
        for half in range(2):
            zz = z[:, half * LANES:(half + 1) * LANES]
            r = zz * cos + pltpu.roll(zz, LANES // 2, 1) * sin
            c0 = col + half * LANES
            if c0 < qk_cols // 2:
                r = r * (ATTN_HEAD_DIM ** -0.5)
            qk_ref[:, c0:c0 + LANES] = r.astype(BF16)
    v_ref[...] = jnp.dot(a, w_ref[:, qk_cols:qk_cols + v_cols], preferred_element_type=F32).astype(BF16)
    c_ref[...] = jnp.dot(a, w_ref[:, qk_cols + v_cols:], preferred_element_type=F32)


def _in_proj(x2, g, w_perm, cos_t, sin_t, seq, tm):
    T, D = x2.shape
    n_cols = w_perm.shape[1]
    qk_cols = 2 * N_ATTN_HEADS * 2 * ATTN_HEAD_DIM
    v_cols = N_ATTN_HEADS * 2 * ATTN_HEAD_DIM
    c_cols = n_cols - qk_cols - v_cols
    spt = seq // tm
    return pl.pallas_call(
        functools.partial(_inproj_kernel, qk_cols=qk_cols, v_cols=v_cols),
        grid=(T // tm,),
        in_specs=[
            pl.BlockSpec((tm, D), lambda i: (i, 0)),
            pl.BlockSpec((1, D), lambda i: (0, 0)),
            pl.BlockSpec((D, n_cols), lambda i: (0, 0)),
            pl.BlockSpec((tm, LANES), lambda i: (i % spt, 0)),
            pl.BlockSpec((tm, LANES), lambda i: (i % spt, 0)),
        ],
        out_specs=[
            pl.BlockSpec((tm, qk_cols), lambda i: (i, 0)),
            pl.BlockSpec((tm, v_cols), lambda i: (i, 0)),
            pl.BlockSpec((tm, c_cols), lambda i: (i, 0)),
        ],
        out_shape=[
            jax.ShapeDtypeStruct((T, qk_cols), BF16),
            jax.ShapeDtypeStruct((T, v_cols), BF16),
            jax.ShapeDtypeStruct((T, c_cols), F32),
        ],
        compiler_params=_params(("parallel",)),
        name="in_proj",
    )(x2, g, w_perm, cos_t, sin_t)


def _attn_kernel(q_ref, k_ref, v_ref, lam_ref, g_ref, o_ref, m_sc, l_sc, acc_sc, *, tq, lambda_init):
    i = pl.program_id(2)
    q = q_ref[...]
    lane = lax.broadcasted_iota(jnp.int32, (1, LANES), 1)
    map0 = (lane // (ATTN_HEAD_DIM // 2)) % 2 == 0
    zero = jnp.zeros_like(q)
    qs = jnp.concatenate([jnp.where(map0, q, zero), jnp.where(map0, zero, q)], axis=0)

    m_sc[...] = jnp.full(m_sc.shape, NEG_INF, F32)
    l_sc[...] = jnp.zeros(l_sc.shape, F32)
    acc_sc[...] = jnp.zeros(acc_sc.shape, F32)

    def step(start, masked):
        k = k_ref[pl.ds(start, tq), :]
        v = v_ref[pl.ds(start, tq), :]
        s = lax.dot_general(qs, k, (((1,), (1,)), ((), ())), preferred_element_type=F32)
        if masked:
            row = lax.broadcasted_iota(jnp.int32, (2 * tq, tq), 0)
            col = lax.broadcasted_iota(jnp.int32, (2 * tq, tq), 1)
            s = jnp.where(col <= row % tq, s, NEG_INF)
        m_prev = m_sc[...]
        m_new = jnp.maximum(m_prev, jnp.max(s, axis=1, keepdims=True))
        alpha = jnp.exp(m_prev - m_new)
        p = jnp.exp(s - m_new)
        l_sc[...] = alpha * l_sc[...] + jnp.sum(p, axis=1, keepdims=True)
        acc_sc[...] = alpha * acc_sc[...] + jnp.dot(p.astype(BF16), v, preferred_element_type=F32)
        m_sc[...] = m_new

    def body(j, carry):
        step(pl.multiple_of(j * tq, tq), False)
        return carry

    lax.fori_loop(0, i, body, 0)
    step(pl.multiple_of(i * tq, tq), True)

    lv = lam_ref[...]
    lam = (jnp.exp(jnp.sum(lv[0:1] * lv[1:2], axis=1, keepdims=True))
           - jnp.exp(jnp.sum(lv[2:3] * lv[3:4], axis=1, keepdims=True)) + lambda_init)
    o = acc_sc[0:tq, :] / l_sc[0:tq, :] - lam * (acc_sc[tq:, :] / l_sc[tq:, :])
    o_ref[...] = (_rms(o, g_ref[...]) * (1.0 - lambda_init)).astype(o_ref.dtype)


def _diff_attn(qk, v, lamv, subln_g, batch, seq, tq, lambda_init):
    T = qk.shape[0]
    H = N_ATTN_HEADS
    nq = seq // tq
    return pl.pallas_call(
        functools.partial(_attn_kernel, tq=tq, lambda_init=lambda_init),
        grid=(batch, H, nq),
        in_specs=[
            pl.BlockSpec((tq, LANES), lambda b, h, i: (b * nq + i, h)),
            pl.BlockSpec((seq, LANES), lambda b, h, i: (b, H + h)),
            pl.BlockSpec((seq, LANES), lambda b, h, i: (b, h)),
            pl.BlockSpec((4, ATTN_HEAD_DIM), lambda b, h, i: (0, 0)),
            pl.BlockSpec((1, LANES), lambda b, h, i: (0, 0)),
        ],
        out_specs=pl.BlockSpec((tq, LANES), lambda b, h, i: (b * nq + i, h)),
        out_shape=jax.ShapeDtypeStruct((T, H * LANES), BF16),
        scratch_shapes=[
            pltpu.VMEM((2 * tq, 1), F32),
            pltpu.VMEM((2 * tq, 1), F32),
            pltpu.VMEM((2 * tq, LANES), F32),
        ],
        compiler_params=_params(("parallel", "parallel", "arbitrary")),
        name="diff_attn",
    )(qk, qk, v, lamv, subln_g)


def _conv_kernel(cur_ref, prev_ref, w_ref, b_ref, g_ref, beta_ref, o_ref, u_sc, *, ts, tiles_per_seq, width, rows):
    i = pl.program_id(0)
    first = (i % tiles_per_seq) == 0
    pc = prev_ref[...]
    up = pc[:, :width] * jax.nn.sigmoid(pc[:, width:])
    u_sc[0:CONV_HALO, :] = jnp.where(first, jnp.zeros_like(up), up)
    c = cur_ref[...]
    u_sc[CONV_HALO:, :] = c[:, :width] * jax.nn.sigmoid(c[:, width:])
    bias = b_ref[...]
    gam = g_ref[...]
    beta = beta_ref[...]
    shift = CONV_HALO - (CONV_KERNEL - 1)

    def chunk(r, carry):
        r0 = pl.multiple_of(r * rows, rows)
        acc = jnp.broadcast_to(bias, (rows, width))
        win = u_sc[pl.ds(r0, rows + CONV_HALO), :]
        for b in range(8):
            taps = [j for j in range(CONV_KERNEL) if (shift + j) % 8 == b]
            if not taps:
                continue
            hi = max(shift + j for j in taps) - b
            wb = win[b:b + hi + rows, :]
            for j in taps:
                a8 = shift + j - b
                acc = acc + w_ref[j:j + 1, :] * wb[a8:a8 + rows, :]
        mu = jnp.mean(acc, axis=-1, keepdims=True)
        d = acc - mu
        var = jnp.mean(d * d, axis=-1, keepdims=True)
        y = d * lax.rsqrt(var + EPS) * gam + beta
        o_ref[pl.ds(r0, rows), :] = (y * jax.nn.sigmoid(y)).astype(o_ref.dtype)
        return carry

    lax.fori_loop(0, ts // rows, chunk, 0)


def _conv_module(cvg, w, b, ln_g, ln_b, seq, ts):
    T, two_w = cvg.shape
    width = two_w // 2
    rows = 32
    hb = ts // CONV_HALO
    return pl.pallas_call(
        functools.partial(_conv_kernel, ts=ts, tiles_per_seq=seq // ts, width=width, rows=rows),
        grid=(T // ts,),
        in_specs=[
            pl.BlockSpec((ts, two_w), lambda i: (i, 0)),
            pl.BlockSpec((CONV_HALO, two_w), lambda i: (jnp.maximum(i * hb - 1, 0), 0)),
            pl.BlockSpec((CONV_KERNEL, width), lambda i: (0, 0)),
            pl.BlockSpec((1, width), lambda i: (0, 0)),
            pl.BlockSpec((1, width), lambda i: (0, 0)),
            pl.BlockSpec((1, width), lambda i: (0, 0)),
        ],
        out_specs=pl.BlockSpec((ts, width), lambda i: (i, 0)),
        out_shape=jax.ShapeDtypeStruct((T, width), BF16),
        scratch_shapes=[pltpu.VMEM((ts + CONV_HALO, width), F32)],
        compiler_params=_params(("parallel",)),
        name="conv_module",
    )(cvg, cvg, w, b, ln_g, ln_b)


def _top_rows(s, n_pick, payload=None):
    n = s.shape[0]
    iota = lax.broadcasted_iota(jnp.int32, s.shape, 0)
    vals, picks = [], []
    for _ in range(n_pick):
        mx = jnp.max(s, axis=0, keepdims=True)
        pos = jnp.min(jnp.where(s == mx, iota, n), axis=0, keepdims=True)
        sel = iota == pos
        vals.append(mx)
        if payload is None:
            picks.append(pos)
        else:
            picks.append(jnp.sum(jnp.where(sel, payload, 0), axis=0, keepdims=True))
        s = jnp.where(sel, NEG_INF, s)
    return vals, picks


def _route_kernel(x_ref, at_ref, cv_ref, wo_ref, g_ref, wq_ref, keys_ref, h_ref, m_ref, e_ref, gt_ref, *, aw):
    h1 = (x_ref[...]
          + jnp.dot(at_ref[...], wo_ref[0:aw, :], preferred_element_type=F32)
          + jnp.dot(cv_ref[...], wo_ref[aw:, :], preferred_element_type=F32))
    h_ref[...] = h1
    m = _rms(h1, g_ref[...])
    m_ref[...] = m
    mb = m.astype(BF16)
    pairs = [(i, j) for i in range(PEER_TOPK) for j in range(PEER_TOPK) if (i + 1) * (j + 1) <= PEER_TOPK]
    n_pad = -len(pairs) % 8
    experts, gates = [], []
    for hd in range(N_PEER_HEADS):
        sv, si = [], []
        for c in range(2):
            c0 = (hd * 2 + c) * PEER_HALF
            q = jnp.dot(mb, wq_ref[:, c0:c0 + PEER_HALF], preferred_element_type=F32).astype(BF16)
            st = lax.dot_general(keys_ref[c], q, (((1,), (1,)), ((), ())), preferred_element_type=F32)
            v_, i_ = _top_rows(st, PEER_TOPK)
            sv.append(v_)
            si.append(i_)
        tm = sv[0][0].shape[1]
        cand = jnp.concatenate([sv[0][i] + sv[1][j] for i, j in pairs]
                               + [jnp.full((n_pad, tm), NEG_INF, F32)], axis=0)
        cexp = jnp.concatenate([si[0][i] * N_KEYS + si[1][j] for i, j in pairs]
                               + [jnp.zeros((n_pad, tm), jnp.int32)], axis=0)
        cs, ce = _top_rows(cand, PEER_TOPK, payload=cexp)
        ex = [jnp.exp(c_ - cs[0]) for c_ in cs]
        den = ex[0]
        for e_ in ex[1:]:
            den = den + e_
        experts += ce
        gates += [e_ / den for e_ in ex]
    e_ref[...] = jnp.concatenate(experts, axis=0).T
    gt_ref[...] = jnp.concatenate(gates, axis=0)


def _route(x2, attn, conv, w_out, g, wq, keys, tm):
    T, D = x2.shape
    aw = attn.shape[1]
    cw = conv.shape[1]
    qc = wq.shape[1]
    hk = N_PEER_HEADS * PEER_TOPK
    return pl.pallas_call(
        functools.partial(_route_kernel, aw=aw),
        grid=(T // tm,),
        in_specs=[
            pl.BlockSpec((tm, D), lambda i: (i, 0)),
            pl.BlockSpec((tm, aw), lambda i: (i, 0)),
            pl.BlockSpec((tm, cw), lambda i: (i, 0)),
            pl.BlockSpec((aw + cw, D), lambda i: (0, 0)),
            pl.BlockSpec((1, D), lambda i: (0, 0)),
            pl.BlockSpec((D, qc), lambda i: (0, 0)),
            pl.BlockSpec((2, N_KEYS, PEER_HALF), lambda i: (0, 0, 0)),
        ],
        out_specs=[
            pl.BlockSpec((tm, D), lambda i: (i, 0)),
            pl.BlockSpec((tm, D), lambda i: (i, 0)),
            pl.BlockSpec((tm, hk), lambda i: (i, 0)),
            pl.BlockSpec((hk, tm), lambda i: (0, i)),
        ],
        out_shape=[
            jax.ShapeDtypeStruct((T, D), F32),
            jax.ShapeDtypeStruct((T, D), F32),
            jax.ShapeDtypeStruct((T, hk), jnp.int32),
            jax.ShapeDtypeStruct((hk, T), F32),
        ],
        compiler_params=_params(("parallel",)),
        name="route",
    )(x2, attn, conv, w_out, g, wq, keys)


def _table_rows(tab):
    n, d = tab.shape
    return tab.astype(BF16).reshape(n, d // LANES, LANES)


def _peer_a_kernel(idx_ref, m_ref, gt_ref, tab_ref, h_ref, p_sc, g_sc, *, tb, hk, sub, unroll):
    def token(t, carry):
        x = m_ref[t]
        ids = idx_ref.at[t]
        for k in range(hk):
            p_sc[k * sub:(k + 1) * sub, :] = tab_ref[ids[k]].astype(F32) * x
        g = p_sc[pl.ds(0, hk, stride=sub), :]
        for s in range(1, sub):
            g = g + p_sc[pl.ds(s, hk, stride=sub), :]
        g_sc[t] = g
        return carry

    lax.fori_loop(0, tb, token, 0)

    lane = lax.broadcasted_iota(jnp.int32, (hk, tb), 1)

    def lane_sums(c, a_t):
        for j in range(unroll):
            t = c * unroll + j
            a_t = jnp.where(lane == t, jnp.sum(g_sc[t], axis=1, keepdims=True), a_t)
        return a_t

    a = lax.fori_loop(0, tb // unroll, lane_sums, jnp.zeros((hk, tb), F32))
    h_ref[...] = 0.5 * a * (1.0 + lax.erf(a * (2.0 ** -0.5))) * gt_ref[...]


def _peer_a(idx, m3, gates, tab, tb):
    T, hk = idx.shape
    sub = tab.shape[1]
    return pl.pallas_call(
        functools.partial(_peer_a_kernel, tb=tb, hk=hk, sub=sub, unroll=8),
        grid=(T // tb,),
        in_specs=[
            pl.BlockSpec((tb, hk), lambda i: (i, 0), memory_space=pltpu.SMEM),
            pl.BlockSpec((tb, sub, LANES), lambda i: (i, 0, 0)),
            pl.BlockSpec((hk, tb), lambda i: (0, i)),
            pl.BlockSpec(tab.shape, lambda i: (0, 0, 0), pipeline_mode=pl.Buffered(1)),
        ],
        out_specs=pl.BlockSpec((hk, tb), lambda i: (0, i)),
        out_shape=jax.ShapeDtypeStruct((hk, T), F32),
        scratch_shapes=[pltpu.VMEM((hk * sub, LANES), F32), pltpu.VMEM((tb, hk, LANES), F32)],
        compiler_params=_params(("arbitrary",)),
        name="peer_a",
    )(idx, m3, gates, tab)


def _peer_y_kernel(idx_ref, h_ref, tab_ref, y_ref, hb_sc, *, tb, hk, n_acc):
    lane = lax.broadcasted_iota(jnp.int32, (hk, tb), 1)

    def token(t, carry):
        w_col = jnp.sum(jnp.where(lane == t, h_ref[...], 0.0), axis=1, keepdims=True)
        hb_sc[...] = jnp.broadcast_to(w_col, hb_sc.shape)
        ids = idx_ref.at[t]
        acc = [None] * n_acc
        for k in range(hk):
            term = hb_sc[k:k + 1, :] * tab_ref[ids[k]].astype(F32)
            acc[k % n_acc] = term if acc[k % n_acc] is None else acc[k % n_acc] + term
        total = acc[0]
        for a in range(1, n_acc):
            total = total + acc[a]
        y_ref[t] = total
        return carry

    lax.fori_loop(0, tb, token, 0)


def _peer_y(idx, h_t, tab, tb):
    T, hk = idx.shape
    sub = tab.shape[1]
    return pl.pallas_call(
        functools.partial(_peer_y_kernel, tb=tb, hk=hk, n_acc=4),
        grid=(T // tb,),
        in_specs=[
            pl.BlockSpec((tb, hk), lambda i: (i, 0), memory_space=pltpu.SMEM),
            pl.BlockSpec((hk, tb), lambda i: (0, i)),
            pl.BlockSpec(tab.shape, lambda i: (0, 0, 0), pipeline_mode=pl.Buffered(1)),
        ],
        out_specs=pl.BlockSpec((tb, sub, LANES), lambda i: (i, 0, 0)),
        out_shape=jax.ShapeDtypeStruct((T, sub, LANES), F32),
        scratch_shapes=[pltpu.VMEM((hk, LANES), F32)],
        compiler_params=_params(("arbitrary",)),
        name="peer_y",
    )(idx, h_t, tab)


def _final_kernel(h_ref, y_ref, p_ref, wp_ref, gp_ref, wg_ref, gf_ref, o_ref, *, apply_final):
    h2 = h_ref[...] + y_ref[...]
    e = jnp.dot(p_ref[...].astype(BF16), wp_ref[...], preferred_element_type=F32)
    n = _rms(h2, gp_ref[...]).astype(BF16)
    gate = jax.nn.sigmoid(jnp.dot(n, wg_ref[...], preferred_element_type=F32))
    h3 = h2 + e * gate
    o_ref[...] = _rms(h3, gf_ref[...]) if apply_final else h3


def _ple(h1, y, p2, w_proj, g_ple, w_gate, g_final, apply_final, tm):
    T, D = h1.shape
    pd = p2.shape[1]
    return pl.pallas_call(
        functools.partial(_final_kernel, apply_final=apply_final),
        grid=(T // tm,),
        in_specs=[
            pl.BlockSpec((tm, D), lambda i: (i, 0)),
            pl.BlockSpec((tm, D), lambda i: (i, 0)),
            pl.BlockSpec((tm, pd), lambda i: (i, 0)),
            pl.BlockSpec((pd, D), lambda i: (0, 0)),
            pl.BlockSpec((1, D), lambda i: (0, 0)),
            pl.BlockSpec((D, D), lambda i: (0, 0)),
            pl.BlockSpec((1, D), lambda i: (0, 0)),
        ],
        out_specs=pl.BlockSpec((tm, D), lambda i: (i, 0)),
        out_shape=jax.ShapeDtypeStruct((T, D), F32),
        compiler_params=_params(("parallel",)),
        name="ple_final",
    )(h1, y, p2, w_proj, g_ple, w_gate, g_final)


def _qk_permutation():
    half = ATTN_HEAD_DIM // 2
    perm = np.zeros(N_ATTN_HEADS * LANES, np.int32)
    for h in range(N_ATTN_HEADS):
        for c in range(2):
            for d in range(ATTN_HEAD_DIM):
                perm[h * LANES + (d // half) * 2 * half + c * half + d % half] = h * LANES + c * ATTN_HEAD_DIM + d
    return perm


def _rope_tables(seq):
    half = ATTN_HEAD_DIM // 2
    inv_freq = 1.0 / (ROPE_THETA ** (jnp.arange(half, dtype=F32) * 2.0 / ATTN_HEAD_DIM))
    ang = jnp.arange(seq).astype(F32)[:, None] * inv_freq[None, :]
    cos, sin = jnp.cos(ang), jnp.sin(ang)
    return jnp.tile(cos, (1, 4)), jnp.concatenate([-sin, -sin, sin, sin], axis=1)


def kernel(x, p, attn_norm_g, w_in, lambda_q1, lambda_k1, lambda_q2, lambda_k2, subln_g, conv_w, conv_b,
           conv_ln_g, conv_ln_b, w_out, ffn_norm_g, peer_wq, peer_keys, peer_u, peer_v, ple_norm_g,
           ple_w_gate, ple_w_proj, final_norm_g):
    B, S, D = x.shape
    T = B * S
    depth = w_in.shape[0]
    assert depth >= 1
    tm = min(512, S)
    tq = min(512, S)
    tr = min(256, S)
    tb = min(128, T)
    qk_half = N_ATTN_HEADS * 2 * ATTN_HEAD_DIM

    perm = _qk_permutation()
    col_order = np.concatenate([perm, qk_half + perm, np.arange(2 * qk_half, w_in.shape[2])])
    cos_t, sin_t = _rope_tables(S)
    row = lambda a: a.reshape(1, -1).astype(F32)

    h = x.reshape(T, D)
    for l in range(depth):
        lambda_init = 0.8 - 0.6 * math.exp(-0.3 * l)
        w_perm = w_in[l][:, col_order].astype(BF16)
        qk, v, cvg = _in_proj(h, row(attn_norm_g[l]), w_perm, cos_t, sin_t, S, tm)
        lamv = jnp.stack([lambda_q1[l], lambda_k1[l], lambda_q2[l], lambda_k2[l]]).astype(F32)
        attn = _diff_attn(qk, v, lamv, row(subln_g[l]), B, S, tq, lambda_init)
        conv = _conv_module(cvg, conv_w[l], row(conv_b[l]), row(conv_ln_g[l]), row(conv_ln_b[l]), S, tm)
        h1, m, experts, gates = _route(h, attn, conv, w_out[l].astype(BF16), row(ffn_norm_g[l]),
                                       peer_wq[l].astype(BF16), peer_keys[l].astype(BF16), tr)
        hw = _peer_a(experts, m.reshape(T, D // LANES, LANES), gates, _table_rows(peer_u[l]), tb)
        y = _peer_y(experts, hw, _table_rows(peer_v[l]), tb).reshape(T, D)
        h = _ple(h1, y, p[l].reshape(T, -1), ple_w_proj[l].astype(BF16), row(ple_norm_g[l]),
                 ple_w_gate[l].astype(BF16), row(final_norm_g), l == depth - 1, tm)
    return h.reshape(B, S, D)
```

```python
import functools
import math

import numpy as np
import jax
import jax.numpy as jnp
from jax import lax
from jax.experimental import pallas as pl
from jax.experimental.pallas import tpu as pltpu

EPS = 1e-6
ROPE_THETA = 10000.0
ATTN_HEAD_DIM = 64
N_ATTN_HEADS = 4
CONV_KERNEL = 31
N_PEER_HEADS = 8
N_KEYS = 128
PEER_TOPK = 16
PEER_HALF = 128

LANES = 128
CONV_HALO = 32
VMEM_LIMIT = 56 * 1024 * 1024

F32 = jnp.float32
BF16 = jnp.bfloat16
NEG_INF = float("-inf")


def _rms(x, g):
    ms = jnp.mean(x * x, axis=-1, keepdims=True)
    return x * lax.rsqrt(ms + EPS) * g


def _params(sem):
    return pltpu.CompilerParams(dimension_semantics=sem, vmem_limit_bytes=VMEM_LIMIT)


def _inproj_kernel(x_ref, g_ref, w_ref, cos_ref, sin_ref, qk_ref, v_ref, c_ref, *, qk_cols, v_cols):
    a = _rms(x_ref[...], g_ref[...]).astype(BF16)
    cos = cos_ref[...]
    sin = sin_ref[...]
    for col in range(0, qk_cols, 2 * LANES):
        z = jnp.dot(a, w_ref[:, col:col + 2 * LANES], preferred_element_type=F32)
        for half in range(2):
            zz = z[:, half * LANES:(half + 1) * LANES]
            r = zz * cos + pltpu.roll(zz, LANES // 2, 1) * sin
            c0 = col + half * LANES
            if c0 < qk_cols // 2:
                r = r * (ATTN_HEAD_DIM ** -0.5)
            qk_ref[:, c0:c0 + LANES] = r.astype(BF16)
    v_ref[...] = jnp.dot(a, w_ref[:, qk_cols:qk_cols + v_cols], preferred_element_type=F32).astype(BF16)
    c_ref[...] = jnp.dot(a, w_ref[:, qk_cols + v_cols:], preferred_element_type=F32)


def _in_proj(x2, g, w_perm, cos_t, sin_t, seq, tm):
    T, D = x2.shape
    n_cols = w_perm.shape[1]
    qk_cols = 2 * N_ATTN_HEADS * 2 * ATTN_HEAD_DIM
    v_cols = N_ATTN_HEADS * 2 * ATTN_HEAD_DIM
    c_cols = n_cols - qk_cols - v_cols
    spt = seq // tm
    return pl.pallas_call(
        functools.partial(_inproj_kernel, qk_cols=qk_cols, v_cols=v_cols),
        grid=(T // tm,),
        in_specs=[
            pl.BlockSpec((tm, D), lambda i: (i, 0)),
            pl.BlockSpec((1, D), lambda i: (0, 0)),
            pl.BlockSpec((D, n_cols), lambda i: (0, 0)),
            pl.BlockSpec((tm, LANES), lambda i: (i % spt, 0)),
            pl.BlockSpec((tm, LANES), lambda i: (i % spt, 0)),
        ],
        out_specs=[
            pl.BlockSpec((tm, qk_cols), lambda i: (i, 0)),
            pl.BlockSpec((tm, v_cols), lambda i: (i, 0)),
            pl.BlockSpec((tm, c_cols), lambda i: (i, 0)),
        ],
        out_shape=[
            jax.ShapeDtypeStruct((T, qk_cols), BF16),
            jax.ShapeDtypeStruct((T, v_cols), BF16),
            jax.ShapeDtypeStruct((T, c_cols), F32),
        ],
        compiler_params=_params(("parallel",)),
        name="in_proj",
    )(x2, g, w_perm, cos_t, sin_t)


def _attn_kernel(q_ref, k_ref, v_ref, lam_ref, g_ref, o_ref, m_sc, l_sc, acc_sc, *, tq, lambda_init):
    i = pl.program_id(2)
    q = q_ref[...]
    lane = lax.broadcasted_iota(jnp.int32, (1, LANES), 1)
    map0 = (lane // (ATTN_HEAD_DIM // 2)) % 2 == 0
    zero = jnp.zeros_like(q)
    qs = jnp.concatenate([jnp.where(map0, q, zero), jnp.where(map0, zero, q)], axis=0)

    m_sc[...] = jnp.full(m_sc.shape, NEG_INF, F32)
    l_sc[...] = jnp.zeros(l_sc.shape, F32)
    acc_sc[...] = jnp.zeros(acc_sc.shape, F32)

    def step(start, masked):
        k = k_ref[pl.ds(start, tq), :]
        v = v_ref[pl.ds(start, tq), :]
        s = lax.dot_general(qs, k, (((1,), (1,)), ((), ())), preferred_element_type=F32)
        if masked:
            row = lax.broadcasted_iota(jnp.int32, (2 * tq, tq), 0)
            col = lax.broadcasted_iota(jnp.int32, (2 * tq, tq), 1)
            s = jnp.where(col <= row % tq, s, NEG_INF)
        m_prev = m_sc[...]
        m_new = jnp.maximum(m_prev, jnp.max(s, axis=1, keepdims=True))
        alpha = jnp.exp(m_prev - m_new)
        p = jnp.exp(s - pltpu.repeat(m_new, tq // LANES, axis=1))
        l_sc[...] = alpha * l_sc[...] + jnp.sum(p, axis=1, keepdims=True)
        acc_sc[...] = alpha * acc_sc[...] + jnp.dot(p.astype(BF16), v, preferred_element_type=F32)
        m_sc[...] = m_new

    def body(j, carry):
        step(pl.multiple_of(j * tq, tq), False)
        return carry

    lax.fori_loop(0, i, body, 0)
    step(pl.multiple_of(i * tq, tq), True)

    lv = lam_ref[...]
    lam = (jnp.exp(jnp.sum(lv[0:1] * lv[1:2], axis=1, keepdims=True))
           - jnp.exp(jnp.sum(lv[2:3] * lv[3:4], axis=1, keepdims=True)) + lambda_init)
    o = acc_sc[0:tq, :] / l_sc[0:tq, :] - lam * (acc_sc[tq:, :] / l_sc[tq:, :])
    o_ref[...] = (_rms(o, g_ref[...]) * (1.0 - lambda_init)).astype(o_ref.dtype)


def _diff_attn(qk, v, lamv, subln_g, batch, seq, tq, lambda_init):
    T = qk.shape[0]
    H = N_ATTN_HEADS
    nq = seq // tq
    return pl.pallas_call(
        functools.partial(_attn_kernel, tq=tq, lambda_init=lambda_init),
        grid=(batch, H, nq),
        in_specs=[
            pl.BlockSpec((tq, LANES), lambda b, h, i: (b * nq + i, h)),
            pl.BlockSpec((seq, LANES), lambda b, h, i: (b, H + h)),
            pl.BlockSpec((seq, LANES), lambda b, h, i: (b, h)),
            pl.BlockSpec((4, ATTN_HEAD_DIM), lambda b, h, i: (0, 0)),
            pl.BlockSpec((1, LANES), lambda b, h, i: (0, 0)),
        ],
        out_specs=pl.BlockSpec((tq, LANES), lambda b, h, i: (b * nq + i, h)),
        out_shape=jax.ShapeDtypeStruct((T, H * LANES), BF16),
        scratch_shapes=[
            pltpu.VMEM((2 * tq, LANES), F32),
            pltpu.VMEM((2 * tq, LANES), F32),
            pltpu.VMEM((2 * tq, LANES), F32),
        ],
        compiler_params=_params(("parallel", "parallel", "arbitrary")),
        name="diff_attn",
    )(qk, qk, v, lamv, subln_g)


def _conv_kernel(cur_ref, prev_ref, w_ref, b_ref, g_ref, beta_ref, o_ref, u_sc, *, ts, tiles_per_seq, width, rows):
    i = pl.program_id(0)
    first = (i % tiles_per_seq) == 0
    pc = prev_ref[...]
    up = pc[:, :width] * jax.nn.sigmoid(pc[:, width:])
    u_sc[0:CONV_HALO, :] = jnp.where(first, jnp.zeros_like(up), up)
    c = cur_ref[...]
    u_sc[CONV_HALO:, :] = c[:, :width] * jax.nn.sigmoid(c[:, width:])
    bias = b_ref[...]
    gam = g_ref[...]
    beta = beta_ref[...]
    shift = CONV_HALO - (CONV_KERNEL - 1)

    def chunk(r, carry):
        r0 = pl.multiple_of(r * rows, rows)
        acc = jnp.broadcast_to(bias, (rows, width))
        win = u_sc[pl.ds(r0, rows + CONV_HALO), :]
        for b in range(8):
            taps = [j for j in range(CONV_KERNEL) if (shift + j) % 8 == b]
            if not taps:
                continue
            hi = max(shift + j for j in taps) - b
            wb = win[b:b + hi + rows, :]
            for j in taps:
                a8 = shift + j - b
                acc = acc + w_ref[j:j + 1, :] * wb[a8:a8 + rows, :]
        mu = jnp.mean(acc, axis=-1, keepdims=True)
        d = acc - mu
        var = jnp.mean(d * d, axis=-1, keepdims=True)
        y = d * lax.rsqrt(var + EPS) * gam + beta
        o_ref[pl.ds(r0, rows), :] = (y * jax.nn.sigmoid(y)).astype(o_ref.dtype)
        return carry

    lax.fori_loop(0, ts // rows, chunk, 0)


def _conv_module(cvg, w, b, ln_g, ln_b, seq, ts):
    T, two_w = cvg.shape
    width = two_w // 2
    rows = 32
    hb = ts // CONV_HALO
    return pl.pallas_call(
        functools.partial(_conv_kernel, ts=ts, tiles_per_seq=seq // ts, width=width, rows=rows),
        grid=(T // ts,),
        in_specs=[
            pl.BlockSpec((ts, two_w), lambda i: (i, 0)),
            pl.BlockSpec((CONV_HALO, two_w), lambda i: (jnp.maximum(i * hb - 1, 0), 0)),
            pl.BlockSpec((CONV_KERNEL, width), lambda i: (0, 0)),
            pl.BlockSpec((1, width), lambda i: (0, 0)),
            pl.BlockSpec((1, width), lambda i: (0, 0)),
            pl.BlockSpec((1, width), lambda i: (0, 0)),
        ],
        out_specs=pl.BlockSpec((ts, width), lambda i: (i, 0)),
        out_shape=jax.ShapeDtypeStruct((T, width), BF16),
        scratch_shapes=[pltpu.VMEM((ts + CONV_HALO, width), F32)],
        compiler_params=_params(("parallel",)),
        name="conv_module",
    )(cvg, cvg, w, b, ln_g, ln_b)


def _top_rows(s, n_pick, payload=None):
    n = s.shape[0]
    iota = lax.broadcasted_iota(jnp.int32, s.shape, 0).astype(F32)
    vals, picks = [], []
    for _ in range(n_pick):
        mx = jnp.max(s, axis=0, keepdims=True)
        pos = jnp.min(jnp.where(s == mx, iota, float(n)), axis=0, keepdims=True)
        sel = iota == pos
        vals.append(mx)
        if payload is None:
            picks.append(pos.astype(jnp.int32))
        else:
            picks.append(jnp.sum(jnp.where(sel, payload, 0), axis=0, keepdims=True))
        s = jnp.where(sel, NEG_INF, s)
    return vals, picks


def _route_kernel(x_ref, at_ref, cv_ref, wo_ref, g_ref, wq_ref, keys_ref, h_ref, m_ref, e_ref, gt_ref, *, aw):
    h1 = (x_ref[...]
          + jnp.dot(at_ref[...], wo_ref[0:aw, :], preferred_element_type=F32)
          + jnp.dot(cv_ref[...], wo_ref[aw:, :], preferred_element_type=F32))
    h_ref[...] = h1
    m = _rms(h1, g_ref[...])
    m_ref[...] = m
    mb = m.astype(BF16)
    pairs = [(i, j) for i in range(PEER_TOPK) for j in range(PEER_TOPK) if (i + 1) * (j + 1) <= PEER_TOPK]
    n_pad = -len(pairs) % 8
    experts, gates = [], []
    for hd in range(N_PEER_HEADS):
        sv, si = [], []
        for c in range(2):
            c0 = (hd * 2 + c) * PEER_HALF
            q = jnp.dot(mb, wq_ref[:, c0:c0 + PEER_HALF], preferred_element_type=F32).astype(BF16)
            st = lax.dot_general(keys_ref[c], q, (((1,), (1,)), ((), ())), preferred_element_type=F32)
            v_, i_ = _top_rows(st, PEER_TOPK)
            sv.append(v_)
            si.append(i_)
        tm = sv[0][0].shape[1]
        cand = jnp.concatenate([sv[0][i] + sv[1][j] for i, j in pairs]
                               + [jnp.full((n_pad, tm), NEG_INF, F32)], axis=0)
        cexp = jnp.concatenate([si[0][i] * N_KEYS + si[1][j] for i, j in pairs]
                               + [jnp.zeros((n_pad, tm), jnp.int32)], axis=0)
        cs, ce = _top_rows(cand, PEER_TOPK, payload=cexp)
        ex = [jnp.exp(c_ - cs[0]) for c_ in cs]
        den = ex[0]
        for e_ in ex[1:]:
            den = den + e_
        experts += ce
        gates += [e_ / den for e_ in ex]
    e_ref[...] = jnp.concatenate(experts, axis=0).T
    gt_ref[...] = jnp.concatenate(gates, axis=0)


def _route(x2, attn, conv, w_out, g, wq, keys, tm):
    T, D = x2.shape
    aw = attn.shape[1]
    cw = conv.shape[1]
    qc = wq.shape[1]
    hk = N_PEER_HEADS * PEER_TOPK
    return pl.pallas_call(
        functools.partial(_route_kernel, aw=aw),
        grid=(T // tm,),
        in_specs=[
            pl.BlockSpec((tm, D), lambda i: (i, 0)),
            pl.BlockSpec((tm, aw), lambda i: (i, 0)),
            pl.BlockSpec((tm, cw), lambda i: (i, 0)),
            pl.BlockSpec((aw + cw, D), lambda i: (0, 0)),
            pl.BlockSpec((1, D), lambda i: (0, 0)),
            pl.BlockSpec((D, qc), lambda i: (0, 0)),
            pl.BlockSpec((2, N_KEYS, PEER_HALF), lambda i: (0, 0, 0)),
        ],
        out_specs=[
            pl.BlockSpec((tm, D), lambda i: (i, 0)),
            pl.BlockSpec((tm, D), lambda i: (i, 0)),
            pl.BlockSpec((tm, hk), lambda i: (i, 0)),
            pl.BlockSpec((hk, tm), lambda i: (0, i)),
        ],
        out_shape=[
            jax.ShapeDtypeStruct((T, D), F32),
            jax.ShapeDtypeStruct((T, D), F32),
            jax.ShapeDtypeStruct((T, hk), jnp.int32),
            jax.ShapeDtypeStruct((hk, T), F32),
        ],
        compiler_params=_params(("parallel",)),
        name="route",
    )(x2, attn, conv, w_out, g, wq, keys)


def _table_rows(tab):
    n, d = tab.shape
    return tab.astype(BF16).reshape(n, d // LANES, LANES)


def _peer_a_kernel(idx_ref, m_ref, gt_ref, tab_ref, h_ref, p_sc, g_sc, *, tb, hk, sub):
    lane = lax.broadcasted_iota(jnp.int32, (hk, tb), 1)

    def place(a_t, t):
        return jnp.where(lane == t, jnp.sum(g_sc[t], axis=1, keepdims=True), a_t)

    g_sc[0] = jnp.zeros(g_sc.shape[1:], F32)

    def token(t, a_t):
        a_t = place(a_t, jnp.maximum(t - 1, 0))
        x = m_ref[t]
        ids = idx_ref.at[t]
        for k in range(hk):
            p_sc[k * sub:(k + 1) * sub, :] = tab_ref[ids[k]].astype(F32) * x
        g = p_sc[pl.ds(0, hk, stride=sub), :]
        for s in range(1, sub):
            g = g + p_sc[pl.ds(s, hk, stride=sub), :]
        g_sc[t] = g
        return a_t

    a = place(lax.fori_loop(0, tb, token, jnp.zeros((hk, tb), F32)), tb - 1)
    h_ref[...] = 0.5 * a * (1.0 + lax.erf(a * (2.0 ** -0.5))) * gt_ref[...]


def _peer_a(idx, m3, gates, tab, tb):
    T, hk = idx.shape
    sub = tab.shape[1]
    return pl.pallas_call(
        functools.partial(_peer_a_kernel, tb=tb, hk=hk, sub=sub),
        grid=(T // tb,),
        in_specs=[
            pl.BlockSpec((tb, hk), lambda i: (i, 0), memory_space=pltpu.SMEM),
            pl.BlockSpec((tb, sub, LANES), lambda i: (i, 0, 0)),
            pl.BlockSpec((hk, tb), lambda i: (0, i)),
            pl.BlockSpec(tab.shape, lambda i: (0, 0, 0), pipeline_mode=pl.Buffered(1)),
        ],
        out_specs=pl.BlockSpec((hk, tb), lambda i: (0, i)),
        out_shape=jax.ShapeDtypeStruct((hk, T), F32),
        scratch_shapes=[pltpu.VMEM((hk * sub, LANES), F32), pltpu.VMEM((tb, hk, LANES), F32)],
        compiler_params=_params(("arbitrary",)),
        name="peer_a",
    )(idx, m3, gates, tab)


def _peer_y_kernel(idx_ref, h_ref, tab_ref, y_ref, wa_sc, wb_sc, *, tb, hk, tu, n_acc):
    lane = lax.broadcasted_iota(jnp.int32, (hk, tb), 1)

    def weights(t0, w_sc):
        h_t = h_ref[...]
        for j in range(tu):
            col = jnp.sum(jnp.where(lane == t0 + j, h_t, 0.0), axis=1, keepdims=True)
            w_sc[j] = jnp.broadcast_to(col, (hk, LANES))

    def rows(t0, w_sc):
        ids = [idx_ref.at[t0 + j] for j in range(tu)]
        acc = [[None] * n_acc for _ in range(tu)]
        for k in range(hk):
            for j in range(tu):
                term = w_sc[j, k:k + 1, :] * tab_ref[ids[j][k]].astype(F32)
                a = k % n_acc
                acc[j][a] = term if acc[j][a] is None else acc[j][a] + term
        for j in range(tu):
            total = acc[j][0]
            for a in range(1, n_acc):
                total = total + acc[j][a]
            y_ref[t0 + j] = total

    weights(0, wa_sc)

    def two_groups(i, carry):
        t0 = i * (2 * tu)
        weights(t0 + tu, wb_sc)
        rows(t0, wa_sc)
        weights(jnp.minimum(t0 + 2 * tu, tb - tu), wa_sc)
        rows(t0 + tu, wb_sc)
        return carry

    lax.fori_loop(0, tb // (2 * tu), two_groups, 0)


def _peer_y(idx, h_t, tab, tb, tu):
    T, hk = idx.shape
    sub = tab.shape[1]
    assert tb % (2 * tu) == 0
    return pl.pallas_call(
        functools.partial(_peer_y_kernel, tb=tb, hk=hk, tu=tu, n_acc=2),
        grid=(T // tb,),
        in_specs=[
            pl.BlockSpec((tb, hk), lambda i: (i, 0), memory_space=pltpu.SMEM),
            pl.BlockSpec((hk, tb), lambda i: (0, i)),
            pl.BlockSpec(tab.shape, lambda i: (0, 0, 0), pipeline_mode=pl.Buffered(1)),
        ],
        out_specs=pl.BlockSpec((tb, sub, LANES), lambda i: (i, 0, 0)),
        out_shape=jax.ShapeDtypeStruct((T, sub, LANES), F32),
        scratch_shapes=[pltpu.VMEM((tu, hk, LANES), F32), pltpu.VMEM((tu, hk, LANES), F32)],
        compiler_params=_params(("arbitrary",)),
        name="peer_y",
    )(idx, h_t, tab)


def _final_kernel(h_ref, y_ref, p_ref, wp_ref, gp_ref, wg_ref, gf_ref, o_ref, *, apply_final):
    h2 = h_ref[...] + y_ref[...]
    e = jnp.dot(p_ref[...].astype(BF16), wp_ref[...], preferred_element_type=F32)
    n = _rms(h2, gp_ref[...]).astype(BF16)
    gate = jax.nn.sigmoid(jnp.dot(n, wg_ref[...], preferred_element_type=F32))
    h3 = h2 + e * gate
    o_ref[...] = _rms(h3, gf_ref[...]) if apply_final else h3


def _ple(h1, y, p2, w_proj, g_ple, w_gate, g_final, apply_final, tm):
    T, D = h1.shape
    pd = p2.shape[1]
    return pl.pallas_call(
        functools.partial(_final_kernel, apply_final=apply_final),
        grid=(T // tm,),
        in_specs=[
            pl.BlockSpec((tm, D), lambda i: (i, 0)),
            pl.BlockSpec((tm, D), lambda i: (i, 0)),
            pl.BlockSpec((tm, pd), lambda i: (i, 0)),
            pl.BlockSpec((pd, D), lambda i: (0, 0)),
            pl.BlockSpec((1, D), lambda i: (0, 0)),
            pl.BlockSpec((D, D), lambda i: (0, 0)),
            pl.BlockSpec((1, D), lambda i: (0, 0)),
        ],
        out_specs=pl.BlockSpec((tm, D), lambda i: (i, 0)),
        out_shape=jax.ShapeDtypeStruct((T, D), F32),
        compiler_params=_params(("parallel",)),
        name="ple_final",
    )(h1, y, p2, w_proj, g_ple, w_gate, g_final)


def _qk_permutation():
    half = ATTN_HEAD_DIM // 2
    perm = np.zeros(N_ATTN_HEADS * LANES, np.int32)
    for h in range(N_ATTN_HEADS):
        for c in range(2):
            for d in range(ATTN_HEAD_DIM):
                perm[h * LANES + (d // half) * 2 * half + c * half + d % half] = h * LANES + c * ATTN_HEAD_DIM + d
    return perm


def _rope_tables(seq):
    half = ATTN_HEAD_DIM // 2
    inv_freq = 1.0 / (ROPE_THETA ** (jnp.arange(half, dtype=F32) * 2.0 / ATTN_HEAD_DIM))
    ang = jnp.arange(seq).astype(F32)[:, None] * inv_freq[None, :]
    cos, sin = jnp.cos(ang), jnp.sin(ang)
    return jnp.tile(cos, (1, 4)), jnp.concatenate([-sin, -sin, sin, sin], axis=1)


def kernel(x, p, attn_norm_g, w_in, lambda_q1, lambda_k1, lambda_q2, lambda_k2, subln_g, conv_w, conv_b,
           conv_ln_g, conv_ln_b, w_out, ffn_norm_g, peer_wq, peer_keys, peer_u, peer_v, ple_norm_g,
           ple_w_gate, ple_w_proj, final_norm_g):
    B, S, D = x.shape
    T = B * S
    depth = w_in.shape[0]
    assert depth >= 1
    tm = min(512, S)
    tq = min(512, S)
    tr = min(256, S)
    tb = min(128, T)
    tu = 2
    qk_half = N_ATTN_HEADS * 2 * ATTN_HEAD_DIM

    perm = _qk_permutation()
    col_order = np.concatenate([perm, qk_half + perm, np.arange(2 * qk_half, w_in.shape[2])])
    cos_t, sin_t = _rope_tables(S)
    row = lambda a: a.reshape(1, -1).astype(F32)

    h = x.reshape(T, D)
    for l in range(depth):
        lambda_init = 0.8 - 0.6 * math.exp(-0.3 * l)
        w_perm = w_in[l][:, col_order].astype(BF16)
        qk, v, cvg = _in_proj(h, row(attn_norm_g[l]), w_perm, cos_t, sin_t, S, tm)
        lamv = jnp.stack([lambda_q1[l], lambda_k1[l], lambda_q2[l], lambda_k2[l]]).astype(F32)
        attn = _diff_attn(qk, v, lamv, row(subln_g[l]), B, S, tq, lambda_init)
        conv = _conv_module(cvg, conv_w[l], row(conv_b[l]), row(conv_ln_g[l]), row(conv_ln_b[l]), S, tm)
        h1, m, experts, gates = _route(h, attn, conv, w_out[l].astype(BF16), row(ffn_norm_g[l]),
                                       peer_wq[l].astype(BF16), peer_keys[l].astype(BF16), tr)
        hw = _peer_a(experts, m.reshape(T, D // LANES, LANES), gates, _table_rows(peer_u[l]), tb)
        y = _peer_y(experts, hw, _table_rows(peer_v[l]), tb, tu).reshape(T, D)
        h = _ple(h1, y, p[l].reshape(T, -1), ple_w_proj[l].astype(BF16), row(ple_norm_g[l]),
                 ple_w_gate[l].astype(BF16), row(final_norm_g), l == depth - 1, tm)
    return h.reshape(B, S, D)
```

```python
import functools
import math

import numpy as np
import jax
import jax.numpy as jnp
from jax import lax
from jax.experimental import pallas as pl
from jax.experimental.pallas import tpu as pltpu
from jax.experimental.pallas import tpu_sc as plsc

EPS = 1e-6
ROPE_THETA = 10000.0
ATTN_HEAD_DIM = 64
N_ATTN_HEADS = 4
CONV_KERNEL = 31
N_PEER_HEADS = 8
N_KEYS = 128
PEER_TOPK = 16
PEER_HALF = 128

LANES = 128
CONV_HALO = 32
VMEM_LIMIT = 56 * 1024 * 1024

F32 = jnp.float32
BF16 = jnp.bfloat16
NEG_INF = float("-inf")


def _rms(x, g):
    ms = jnp.mean(x * x, axis=-1, keepdims=True)
    return x * lax.rsqrt(ms + EPS) * g


def _params(sem):
    return pltpu.CompilerParams(dimension_semantics=sem, vmem_limit_bytes=VMEM_LIMIT)


def _inproj_kernel(x_ref, g_ref, w_ref, cos_ref, sin_ref, qk_ref, v_ref, c_ref, *, qk_cols, v_cols):
    a = _rms(x_ref[...], g_ref[...]).astype(BF16)
    cos = cos_ref[...]
    sin = sin_ref[...]
    for col in range(0, qk_cols, 2 * LANES):
        z = jnp.dot(a, w_ref[:, col:col + 2 * LANES], preferred_element_type=F32)
        for half in range(2):
            zz = z[:, half * LANES:(half + 1) * LANES]
            r = zz * cos + pltpu.roll(zz, LANES // 2, 1) * sin
            c0 = col + half * LANES
            if c0 < qk_cols // 2:
                r = r * (ATTN_HEAD_DIM ** -0.5)
            qk_ref[:, c0:c0 + LANES] = r.astype(BF16)
    v_ref[...] = jnp.dot(a, w_ref[:, qk_cols:qk_cols + v_cols], preferred_element_type=F32).astype(BF16)
    c_ref[...] = jnp.dot(a, w_ref[:, qk_cols + v_cols:], preferred_element_type=F32)


def _in_proj(x2, g, w_perm, cos_t, sin_t, seq, tm):
    T, D = x2.shape
    n_cols = w_perm.shape[1]
    qk_cols = 2 * N_ATTN_HEADS * 2 * ATTN_HEAD_DIM
    v_cols = N_ATTN_HEADS * 2 * ATTN_HEAD_DIM
    c_cols = n_cols - qk_cols - v_cols
    spt = seq // tm
    return pl.pallas_call(
        functools.partial(_inproj_kernel, qk_cols=qk_cols, v_cols=v_cols),
        grid=(T // tm,),
        in_specs=[
            pl.BlockSpec((tm, D), lambda i: (i, 0)),
            pl.BlockSpec((1, D), lambda i: (0, 0)),
            pl.BlockSpec((D, n_cols), lambda i: (0, 0)),
            pl.BlockSpec((tm, LANES), lambda i: (i % spt, 0)),
            pl.BlockSpec((tm, LANES), lambda i: (i % spt, 0)),
        ],
        out_specs=[
            pl.BlockSpec((tm, qk_cols), lambda i: (i, 0)),
            pl.BlockSpec((tm, v_cols), lambda i: (i, 0)),
            pl.BlockSpec((tm, c_cols), lambda i: (i, 0)),
        ],
        out_shape=[
            jax.ShapeDtypeStruct((T, qk_cols), BF16),
            jax.ShapeDtypeStruct((T, v_cols), BF16),
            jax.ShapeDtypeStruct((T, c_cols), F32),
        ],
        compiler_params=_params(("parallel",)),
        name="in_proj",
    )(x2, g, w_perm, cos_t, sin_t)


def _attn_kernel(q_ref, k_ref, v_ref, lam_ref, g_ref, o_ref, m_sc, l_sc, acc_sc, *, tq, lambda_init):
    i = pl.program_id(2)
    q = q_ref[...]
    lane = lax.broadcasted_iota(jnp.int32, (1, LANES), 1)
    map0 = (lane // (ATTN_HEAD_DIM // 2)) % 2 == 0
    zero = jnp.zeros_like(q)
    qs = jnp.concatenate([jnp.where(map0, q, zero), jnp.where(map0, zero, q)], axis=0)

    m_sc[...] = jnp.full(m_sc.shape, NEG_INF, F32)
    l_sc[...] = jnp.zeros(l_sc.shape, F32)
    acc_sc[...] = jnp.zeros(acc_sc.shape, F32)

    def step(start, masked):
        k = k_ref[pl.ds(start, tq), :]
        v = v_ref[pl.ds(start, tq), :]
        s = lax.dot_general(qs, k, (((1,), (1,)), ((), ())), preferred_element_type=F32)
        if masked:
            row = lax.broadcasted_iota(jnp.int32, (2 * tq, tq), 0)
            col = lax.broadcasted_iota(jnp.int32, (2 * tq, tq), 1)
            s = jnp.where(col <= row % tq, s, NEG_INF)
        m_prev = m_sc[...]
        m_new = jnp.maximum(m_prev, jnp.max(s, axis=1, keepdims=True))
        alpha = jnp.exp(m_prev - m_new)
        p = jnp.exp(s - jnp.tile(m_new, (1, tq // LANES)))
        l_sc[...] = alpha * l_sc[...] + jnp.sum(p, axis=1, keepdims=True)
        acc_sc[...] = alpha * acc_sc[...] + jnp.dot(p.astype(BF16), v, preferred_element_type=F32)
        m_sc[...] = m_new

    def body(j, carry):
        step(pl.multiple_of(j * tq, tq), False)
        return carry

    lax.fori_loop(0, i, body, 0)
    step(pl.multiple_of(i * tq, tq), True)

    lv = lam_ref[...]
    lam = (jnp.exp(jnp.sum(lv[0:1] * lv[1:2], axis=1, keepdims=True))
           - jnp.exp(jnp.sum(lv[2:3] * lv[3:4], axis=1, keepdims=True)) + lambda_init)
    o = acc_sc[0:tq, :] / l_sc[0:tq, :] - lam * (acc_sc[tq:, :] / l_sc[tq:, :])
    o_ref[...] = (_rms(o, g_ref[...]) * (1.0 - lambda_init)).astype(o_ref.dtype)


def _diff_attn(qk, v, lamv, subln_g, batch, seq, tq, lambda_init):
    T = qk.shape[0]
    H = N_ATTN_HEADS
    nq = seq // tq
    return pl.pallas_call(
        functools.partial(_attn_kernel, tq=tq, lambda_init=lambda_init),
        grid=(batch, H, nq),
        in_specs=[
            pl.BlockSpec((tq, LANES), lambda b, h, i: (b * nq + i, h)),
            pl.BlockSpec((seq, LANES), lambda b, h, i: (b, H + h)),
            pl.BlockSpec((seq, LANES), lambda b, h, i: (b, h)),
            pl.BlockSpec((4, ATTN_HEAD_DIM), lambda b, h, i: (0, 0)),
            pl.BlockSpec((1, LANES), lambda b, h, i: (0, 0)),
        ],
        out_specs=pl.BlockSpec((tq, LANES), lambda b, h, i: (b * nq + i, h)),
        out_shape=jax.ShapeDtypeStruct((T, H * LANES), BF16),
        scratch_shapes=[
            pltpu.VMEM((2 * tq, LANES), F32),
            pltpu.VMEM((2 * tq, LANES), F32),
            pltpu.VMEM((2 * tq, LANES), F32),
        ],
        compiler_params=_params(("parallel", "parallel", "arbitrary")),
        name="diff_attn",
    )(qk, qk, v, lamv, subln_g)


def _conv_kernel(cur_ref, prev_ref, w_ref, b_ref, g_ref, beta_ref, o_ref, u_sc, *, ts, tiles_per_seq, width, rows):
    i = pl.program_id(0)
    first = (i % tiles_per_seq) == 0
    pc = prev_ref[...]
    up = pc[:, :width] * jax.nn.sigmoid(pc[:, width:])
    u_sc[0:CONV_HALO, :] = jnp.where(first, jnp.zeros_like(up), up)
    c = cur_ref[...]
    u_sc[CONV_HALO:, :] = c[:, :width] * jax.nn.sigmoid(c[:, width:])
    bias = b_ref[...]
    gam = g_ref[...]
    beta = beta_ref[...]
    shift = CONV_HALO - (CONV_KERNEL - 1)

    def chunk(r, carry):
        r0 = pl.multiple_of(r * rows, rows)
        acc = jnp.broadcast_to(bias, (rows, width))
        win = u_sc[pl.ds(r0, rows + CONV_HALO), :]
        for b in range(8):
            taps = [j for j in range(CONV_KERNEL) if (shift + j) % 8 == b]
            if not taps:
                continue
            hi = max(shift + j for j in taps) - b
            wb = win[b:b + hi + rows, :]
            for j in taps:
                a8 = shift + j - b
                acc = acc + w_ref[j:j + 1, :] * wb[a8:a8 + rows, :]
        mu = jnp.mean(acc, axis=-1, keepdims=True)
        d = acc - mu
        var = jnp.mean(d * d, axis=-1, keepdims=True)
        y = d * lax.rsqrt(var + EPS) * gam + beta
        o_ref[pl.ds(r0, rows), :] = (y * jax.nn.sigmoid(y)).astype(o_ref.dtype)
        return carry

    lax.fori_loop(0, ts // rows, chunk, 0)


def _conv_module(cvg, w, b, ln_g, ln_b, seq, ts):
    T, two_w = cvg.shape
    width = two_w // 2
    rows = 32
    hb = ts // CONV_HALO
    return pl.pallas_call(
        functools.partial(_conv_kernel, ts=ts, tiles_per_seq=seq // ts, width=width, rows=rows),
        grid=(T // ts,),
        in_specs=[
            pl.BlockSpec((ts, two_w), lambda i: (i, 0)),
            pl.BlockSpec((CONV_HALO, two_w), lambda i: (jnp.maximum(i * hb - 1, 0), 0)),
            pl.BlockSpec((CONV_KERNEL, width), lambda i: (0, 0)),
            pl.BlockSpec((1, width), lambda i: (0, 0)),
            pl.BlockSpec((1, width), lambda i: (0, 0)),
            pl.BlockSpec((1, width), lambda i: (0, 0)),
        ],
        out_specs=pl.BlockSpec((ts, width), lambda i: (i, 0)),
        out_shape=jax.ShapeDtypeStruct((T, width), BF16),
        scratch_shapes=[pltpu.VMEM((ts + CONV_HALO, width), F32)],
        compiler_params=_params(("parallel",)),
        name="conv_module",
    )(cvg, cvg, w, b, ln_g, ln_b)


def _top_rows(s, n_pick, payload=None):
    n = s.shape[0]
    iota = lax.broadcasted_iota(jnp.int32, s.shape, 0).astype(F32)
    vals, picks = [], []
    for _ in range(n_pick):
        mx = jnp.max(s, axis=0, keepdims=True)
        pos = jnp.min(jnp.where(s == mx, iota, float(n)), axis=0, keepdims=True)
        sel = iota == pos
        vals.append(mx)
        if payload is None:
            picks.append(pos.astype(jnp.int32))
        else:
            picks.append(jnp.sum(jnp.where(sel, payload, 0), axis=0, keepdims=True))
        s = jnp.where(sel, NEG_INF, s)
    return vals, picks


def _route_kernel(x_ref, at_ref, cv_ref, wo_ref, g_ref, wq_ref, keys_ref, h_ref, m_ref, e_ref, gt_ref, *, aw):
    h1 = (x_ref[...]
          + jnp.dot(at_ref[...], wo_ref[0:aw, :], preferred_element_type=F32)
          + jnp.dot(cv_ref[...], wo_ref[aw:, :], preferred_element_type=F32))
    h_ref[...] = h1
    m = _rms(h1, g_ref[...])
    m_ref[...] = m
    mb = m.astype(BF16)
    pairs = [(i, j) for i in range(PEER_TOPK) for j in range(PEER_TOPK) if (i + 1) * (j + 1) <= PEER_TOPK]
    n_pad = -len(pairs) % 8
    experts, gates = [], []
    for hd in range(N_PEER_HEADS):
        sv, si = [], []
        for c in range(2):
            c0 = (hd * 2 + c) * PEER_HALF
            q = jnp.dot(mb, wq_ref[:, c0:c0 + PEER_HALF], preferred_element_type=F32).astype(BF16)
            st = lax.dot_general(keys_ref[c], q, (((1,), (1,)), ((), ())), preferred_element_type=F32)
            v_, i_ = _top_rows(st, PEER_TOPK)
            sv.append(v_)
            si.append(i_)
        tm = sv[0][0].shape[1]
        cand = jnp.concatenate([sv[0][i] + sv[1][j] for i, j in pairs]
                               + [jnp.full((n_pad, tm), NEG_INF, F32)], axis=0)
        cexp = jnp.concatenate([si[0][i] * N_KEYS + si[1][j] for i, j in pairs]
                               + [jnp.zeros((n_pad, tm), jnp.int32)], axis=0)
        cs, ce = _top_rows(cand, PEER_TOPK, payload=cexp)
        ex = [jnp.exp(c_ - cs[0]) for c_ in cs]
        den = ex[0]
        for e_ in ex[1:]:
            den = den + e_
        experts += ce
        gates += [e_ / den for e_ in ex]
    e_ref[...] = jnp.concatenate(experts, axis=0).T
    gt_ref[...] = jnp.concatenate(gates, axis=0)


def _route(x2, attn, conv, w_out, g, wq, keys, tm):
    T, D = x2.shape
    aw = attn.shape[1]
    cw = conv.shape[1]
    qc = wq.shape[1]
    hk = N_PEER_HEADS * PEER_TOPK
    return pl.pallas_call(
        functools.partial(_route_kernel, aw=aw),
        grid=(T // tm,),
        in_specs=[
            pl.BlockSpec((tm, D), lambda i: (i, 0)),
            pl.BlockSpec((tm, aw), lambda i: (i, 0)),
            pl.BlockSpec((tm, cw), lambda i: (i, 0)),
            pl.BlockSpec((aw + cw, D), lambda i: (0, 0)),
            pl.BlockSpec((1, D), lambda i: (0, 0)),
            pl.BlockSpec((D, qc), lambda i: (0, 0)),
            pl.BlockSpec((2, N_KEYS, PEER_HALF), lambda i: (0, 0, 0)),
        ],
        out_specs=[
            pl.BlockSpec((tm, D), lambda i: (i, 0)),
            pl.BlockSpec((tm, D), lambda i: (i, 0)),
            pl.BlockSpec((tm, hk), lambda i: (i, 0)),
            pl.BlockSpec((hk, tm), lambda i: (0, i)),
        ],
        out_shape=[
            jax.ShapeDtypeStruct((T, D), F32),
            jax.ShapeDtypeStruct((T, D), F32),
            jax.ShapeDtypeStruct((T, hk), jnp.int32),
            jax.ShapeDtypeStruct((hk, T), F32),
        ],
        compiler_params=_params(("parallel",)),
        name="route",
    )(x2, attn, conv, w_out, g, wq, keys)


def _table_rows(tab):
    n, d = tab.shape
    return tab.astype(BF16).reshape(n, d // LANES, LANES)


def _peer_a_kernel(idx_ref, m_ref, gt_ref, tab_ref, h_ref, p_sc, g_sc, *, tb, hk, sub):
    lane = lax.broadcasted_iota(jnp.int32, (hk, tb), 1)

    def place(a_t, t):
        return jnp.where(lane == t, jnp.sum(g_sc[t], axis=1, keepdims=True), a_t)

    g_sc[0] = jnp.zeros(g_sc.shape[1:], F32)

    def token(t, a_t):
        a_t = place(a_t, jnp.maximum(t - 1, 0))
        x = m_ref[t]
        ids = idx_ref.at[t]
        for k in range(hk):
            p_sc[k * sub:(k + 1) * sub, :] = tab_ref[ids[k]].astype(F32) * x
        g = p_sc[pl.ds(0, hk, stride=sub), :]
        for s in range(1, sub):
            g = g + p_sc[pl.ds(s, hk, stride=sub), :]
        g_sc[t] = g
        return a_t

    a = place(lax.fori_loop(0, tb, token, jnp.zeros((hk, tb), F32)), tb - 1)
    h_ref[...] = (0.5 * a * (1.0 + lax.erf(a * (2.0 ** -0.5))) * gt_ref[...]).T


def _peer_a(idx, m3, gates, tab, tb, chunk, tc):
    hk = idx.shape[1]
    sub = tab.shape[1]
    b0 = chunk * (tc // tb)
    return pl.pallas_call(
        functools.partial(_peer_a_kernel, tb=tb, hk=hk, sub=sub),
        grid=(tc // tb,),
        in_specs=[
            pl.BlockSpec((tb, hk), lambda i: (b0 + i, 0), memory_space=pltpu.SMEM),
            pl.BlockSpec((tb, sub, LANES), lambda i: (b0 + i, 0, 0)),
            pl.BlockSpec((hk, tb), lambda i: (0, b0 + i)),
            pl.BlockSpec(tab.shape, lambda i: (0, 0, 0), pipeline_mode=pl.Buffered(1)),
        ],
        out_specs=pl.BlockSpec((tb, hk), lambda i: (i, 0)),
        out_shape=jax.ShapeDtypeStruct((tc, hk), F32),
        scratch_shapes=[pltpu.VMEM((hk * sub, LANES), F32), pltpu.VMEM((tb, hk, LANES), F32)],
        compiler_params=_params(("arbitrary",)),
        name="peer_a",
    )(idx, m3, gates, tab)


SC_LANES = 16
SC_WORD_PAIR = 32


def _table_words(tab):
    n, d = tab.shape
    t = tab.astype(BF16).reshape(n, d // SC_WORD_PAIR, 2, SC_LANES).transpose(0, 1, 3, 2)
    return lax.bitcast_convert_type(t.reshape(n, d // 2, 2), jnp.int32)


def _peer_y_sc(idx, h, tab_words, chunk, tc):
    hk = idx.shape[1]
    dw = tab_words.shape[1]
    d = 2 * dw
    info = plsc.get_sparse_core_info()
    nc, ns = info.num_cores, info.num_subcores
    tpw = tc // (nc * ns)
    half = hk // 2
    mesh = plsc.VectorSubcoreMesh(core_axis_name="c", subcore_axis_name="s")

    @functools.partial(
        pl.kernel, mesh=mesh,
        out_type=jax.ShapeDtypeStruct((tc, d), F32),
        scratch_types=[
            pltpu.VMEM((tpw, hk), jnp.int32),
            pltpu.VMEM((tpw, hk), F32),
            pltpu.VMEM((half, dw), jnp.int32),
            pltpu.VMEM((half, dw), jnp.int32),
            pltpu.VMEM((d,), F32),
            pltpu.SemaphoreType.DMA,
            pltpu.SemaphoreType.DMA,
        ],
        compiler_params=pltpu.CompilerParams(needs_layout_passes=False),
        name="peer_y_sc",
    )
    def body(idx_hbm, h_hbm, tab_hbm, y_hbm, ids_v, w_v, rows_a, rows_b, acc_v, sem_a, sem_b):
        base = (lax.axis_index("s") * nc + lax.axis_index("c")) * tpw
        pltpu.sync_copy(idx_hbm.at[pl.ds(chunk * tc + base, tpw)], ids_v)
        pltpu.sync_copy(h_hbm.at[pl.ds(base, tpw)], w_v)

        def gather(i, hf, rows, sem):
            return pltpu.make_async_copy(tab_hbm.at[ids_v.at[i, pl.ds(hf * half, half)]], rows, sem)

        def accumulate(i, hf, rows):
            lanes0 = jnp.zeros((SC_LANES,), jnp.int32)

            @pl.loop(0, half)
            def _(r):
                w = plsc.load_gather(w_v, [lanes0 + i, lanes0 + (hf * half + r)])
                words = [rows[r, pl.ds(c * SC_LANES, SC_LANES)] for c in range(dw // SC_LANES)]
                for c in range(dw // SC_LANES):
                    v = plsc.bitcast(words[c], BF16)
                    lo, hi = plsc.unpack(v, format=plsc.PackFormat.INTERLEAVED, preferred_element_type=F32)
                    plsc.addupdate(acc_v.at[pl.ds(c * SC_WORD_PAIR, SC_LANES)], w * lo)
                    plsc.addupdate(acc_v.at[pl.ds(c * SC_WORD_PAIR + SC_LANES, SC_LANES)], w * hi)

        gather(0, 0, rows_a, sem_a).start()

        @pl.loop(0, tpw)
        def _(i):
            gather(i, 1, rows_b, sem_b).start()
            zero = jnp.zeros((SC_LANES,), F32)
            for c in range(d // SC_LANES):
                acc_v[pl.ds(c * SC_LANES, SC_LANES)] = zero
            gather(i, 0, rows_a, sem_a).wait()
            accumulate(i, 0, rows_a)

            @pl.when(i + 1 < tpw)
            def _():
                gather(i + 1, 0, rows_a, sem_a).start()

            gather(i, 1, rows_b, sem_b).wait()
            accumulate(i, 1, rows_b)
            pltpu.sync_copy(acc_v, y_hbm.at[base + i])

    return body(idx, h, tab_words)


def _peer_y_kernel(idx_ref, h_ref, tab_ref, y_ref, wa_sc, wb_sc, *, tb, hk, tu, n_acc):
    lane = lax.broadcasted_iota(jnp.int32, (hk, tb), 1)

    def weights(t0, w_sc):
        h_t = h_ref[...]
        for j in range(tu):
            col = jnp.sum(jnp.where(lane == t0 + j, h_t, 0.0), axis=1, keepdims=True)
            w_sc[j] = jnp.broadcast_to(col, (hk, LANES))

    def rows(t0, w_sc):
        ids = [idx_ref.at[t0 + j] for j in range(tu)]
        acc = [[None] * n_acc for _ in range(tu)]
        for k in range(hk):
            for j in range(tu):
                term = w_sc[j, k:k + 1, :] * tab_ref[ids[j][k]].astype(F32)
                a = k % n_acc
                acc[j][a] = term if acc[j][a] is None else acc[j][a] + term
        for j in range(tu):
            total = acc[j][0]
            for a in range(1, n_acc):
                total = total + acc[j][a]
            y_ref[t0 + j] = total

    weights(0, wa_sc)

    def two_groups(i, carry):
        t0 = i * (2 * tu)
        weights(t0 + tu, wb_sc)
        rows(t0, wa_sc)
        weights(jnp.minimum(t0 + 2 * tu, tb - tu), wa_sc)
        rows(t0 + tu, wb_sc)
        return carry

    lax.fori_loop(0, tb // (2 * tu), two_groups, 0)


def _peer_y(idx, h_t, tab, tb, tu):
    T, hk = idx.shape
    sub = tab.shape[1]
    assert tb % (2 * tu) == 0
    return pl.pallas_call(
        functools.partial(_peer_y_kernel, tb=tb, hk=hk, tu=tu, n_acc=2),
        grid=(T // tb,),
        in_specs=[
            pl.BlockSpec((tb, hk), lambda i: (i, 0), memory_space=pltpu.SMEM),
            pl.BlockSpec((hk, tb), lambda i: (0, i)),
            pl.BlockSpec(tab.shape, lambda i: (0, 0, 0), pipeline_mode=pl.Buffered(1)),
        ],
        out_specs=pl.BlockSpec((tb, sub, LANES), lambda i: (i, 0, 0)),
        out_shape=jax.ShapeDtypeStruct((T, sub, LANES), F32),
        scratch_shapes=[pltpu.VMEM((tu, hk, LANES), F32), pltpu.VMEM((tu, hk, LANES), F32)],
        compiler_params=_params(("arbitrary",)),
        name="peer_y",
    )(idx, h_t, tab)


def _final_kernel(h_ref, y_ref, p_ref, wp_ref, gp_ref, wg_ref, gf_ref, o_ref, *, apply_final):
    h2 = h_ref[...] + y_ref[...]
    e = jnp.dot(p_ref[...].astype(BF16), wp_ref[...], preferred_element_type=F32)
    n = _rms(h2, gp_ref[...]).astype(BF16)
    gate = jax.nn.sigmoid(jnp.dot(n, wg_ref[...], preferred_element_type=F32))
    h3 = h2 + e * gate
    o_ref[...] = _rms(h3, gf_ref[...]) if apply_final else h3


def _ple(h1, y, p2, w_proj, g_ple, w_gate, g_final, apply_final, tm):
    T, D = h1.shape
    pd = p2.shape[1]
    return pl.pallas_call(
        functools.partial(_final_kernel, apply_final=apply_final),
        grid=(T // tm,),
        in_specs=[
            pl.BlockSpec((tm, D), lambda i: (i, 0)),
            pl.BlockSpec((tm, D), lambda i: (i, 0)),
            pl.BlockSpec((tm, pd), lambda i: (i, 0)),
            pl.BlockSpec((pd, D), lambda i: (0, 0)),
            pl.BlockSpec((1, D), lambda i: (0, 0)),
            pl.BlockSpec((D, D), lambda i: (0, 0)),
            pl.BlockSpec((1, D), lambda i: (0, 0)),
        ],
        out_specs=pl.BlockSpec((tm, D), lambda i: (i, 0)),
        out_shape=jax.ShapeDtypeStruct((T, D), F32),
        compiler_params=_params(("parallel",)),
        name="ple_final",
    )(h1, y, p2, w_proj, g_ple, w_gate, g_final)


def _qk_permutation():
    half = ATTN_HEAD_DIM // 2
    perm = np.zeros(N_ATTN_HEADS * LANES, np.int32)
    for h in range(N_ATTN_HEADS):
        for c in range(2):
            for d in range(ATTN_HEAD_DIM):
                perm[h * LANES + (d // half) * 2 * half + c * half + d % half] = h * LANES + c * ATTN_HEAD_DIM + d
    return perm


def _rope_tables(seq):
    half = ATTN_HEAD_DIM // 2
    inv_freq = 1.0 / (ROPE_THETA ** (jnp.arange(half, dtype=F32) * 2.0 / ATTN_HEAD_DIM))
    ang = jnp.arange(seq).astype(F32)[:, None] * inv_freq[None, :]
    cos, sin = jnp.cos(ang), jnp.sin(ang)
    return jnp.tile(cos, (1, 4)), jnp.concatenate([-sin, -sin, sin, sin], axis=1)


def kernel(x, p, attn_norm_g, w_in, lambda_q1, lambda_k1, lambda_q2, lambda_k2, subln_g, conv_w, conv_b,
           conv_ln_g, conv_ln_b, w_out, ffn_norm_g, peer_wq, peer_keys, peer_u, peer_v, ple_norm_g,
           ple_w_gate, ple_w_proj, final_norm_g):
    B, S, D = x.shape
    T = B * S
    depth = w_in.shape[0]
    assert depth >= 1
    tm = min(512, S)
    tq = min(512, S)
    tr = min(256, S)
    tb = min(128, T)
    tc = min(4096, T)
    qk_half = N_ATTN_HEADS * 2 * ATTN_HEAD_DIM

    perm = _qk_permutation()
    col_order = np.concatenate([perm, qk_half + perm, np.arange(2 * qk_half, w_in.shape[2])])
    cos_t, sin_t = _rope_tables(S)
    row = lambda a: a.reshape(1, -1).astype(F32)

    h = x.reshape(T, D)
    for l in range(depth):
        lambda_init = 0.8 - 0.6 * math.exp(-0.3 * l)
        w_perm = w_in[l][:, col_order].astype(BF16)
        qk, v, cvg = _in_proj(h, row(attn_norm_g[l]), w_perm, cos_t, sin_t, S, tm)
        lamv = jnp.stack([lambda_q1[l], lambda_k1[l], lambda_q2[l], lambda_k2[l]]).astype(F32)
        attn = _diff_attn(qk, v, lamv, row(subln_g[l]), B, S, tq, lambda_init)
        conv = _conv_module(cvg, conv_w[l], row(conv_b[l]), row(conv_ln_g[l]), row(conv_ln_b[l]), S, tm)
        h1, m, experts, gates = _route(h, attn, conv, w_out[l].astype(BF16), row(ffn_norm_g[l]),
                                       peer_wq[l].astype(BF16), peer_keys[l].astype(BF16), tr)
        u_rows, v_words, m3 = _table_rows(peer_u[l]), _table_words(peer_v[l]), m.reshape(T, D // LANES, LANES)
        ys = []
        for c in range(T // tc):
            hw = _peer_a(experts, m3, gates, u_rows, tb, c, tc)
            ys.append(_peer_y_sc(experts, hw, v_words, c, tc))
        y = jnp.concatenate(ys, axis=0)
        h = _ple(h1, y, p[l].reshape(T, -1), ple_w_proj[l].astype(BF16), row(ple_norm_g[l]),
                 ple_w_gate[l].astype(BF16), row(final_norm_g), l == depth - 1, tm)
    return h.reshape(B, S, D)
```

```python
import functools
import math

import numpy as np
import jax
import jax.numpy as jnp
from jax import lax
from jax.experimental import pallas as pl
from jax.experimental.pallas import tpu as pltpu
from jax.experimental.pallas import tpu_sc as plsc

EPS = 1e-6
ROPE_THETA = 10000.0
ATTN_HEAD_DIM = 64
N_ATTN_HEADS = 4
CONV_KERNEL = 31
N_PEER_HEADS = 8
N_KEYS = 128
PEER_TOPK = 16
PEER_HALF = 128

LANES = 128
CONV_HALO = 32
VMEM_LIMIT = 56 * 1024 * 1024

F32 = jnp.float32
BF16 = jnp.bfloat16
NEG_INF = float("-inf")


def _rms(x, g):
    ms = jnp.mean(x * x, axis=-1, keepdims=True)
    return x * lax.rsqrt(ms + EPS) * g


def _params(sem):
    return pltpu.CompilerParams(dimension_semantics=sem, vmem_limit_bytes=VMEM_LIMIT)


def _inproj_kernel(x_ref, g_ref, w_ref, cos_ref, sin_ref, qk_ref, v_ref, c_ref, *, qk_cols, v_cols):
    a = _rms(x_ref[...], g_ref[...]).astype(BF16)
    cos = cos_ref[...]
    sin = sin_ref[...]
    for col in range(0, qk_cols, 2 * LANES):
        z = jnp.dot(a, w_ref[:, col:col + 2 * LANES], preferred_element_type=F32)
        for half in range(2):
            zz = z[:, half * LANES:(half + 1) * LANES]
            r = zz * cos + pltpu.roll(zz, LANES // 2, 1) * sin
            c0 = col + half * LANES
            if c0 < qk_cols // 2:
                r = r * (ATTN_HEAD_DIM ** -0.5)
            qk_ref[:, c0:c0 + LANES] = r.astype(BF16)
    v_ref[...] = jnp.dot(a, w_ref[:, qk_cols:qk_cols + v_cols], preferred_element_type=F32).astype(BF16)
    c_ref[...] = jnp.dot(a, w_ref[:, qk_cols + v_cols:], preferred_element_type=F32)


def _in_proj(x2, g, w_perm, cos_t, sin_t, seq, tm):
    T, D = x2.shape
    n_cols = w_perm.shape[1]
    qk_cols = 2 * N_ATTN_HEADS * 2 * ATTN_HEAD_DIM
    v_cols = N_ATTN_HEADS * 2 * ATTN_HEAD_DIM
    c_cols = n_cols - qk_cols - v_cols
    spt = seq // tm
    return pl.pallas_call(
        functools.partial(_inproj_kernel, qk_cols=qk_cols, v_cols=v_cols),
        grid=(T // tm,),
        in_specs=[
            pl.BlockSpec((tm, D), lambda i: (i, 0)),
            pl.BlockSpec((1, D), lambda i: (0, 0)),
            pl.BlockSpec((D, n_cols), lambda i: (0, 0)),
            pl.BlockSpec((tm, LANES), lambda i: (i % spt, 0)),
            pl.BlockSpec((tm, LANES), lambda i: (i % spt, 0)),
        ],
        out_specs=[
            pl.BlockSpec((tm, qk_cols), lambda i: (i, 0)),
            pl.BlockSpec((tm, v_cols), lambda i: (i, 0)),
            pl.BlockSpec((tm, c_cols), lambda i: (i, 0)),
        ],
        out_shape=[
            jax.ShapeDtypeStruct((T, qk_cols), BF16),
            jax.ShapeDtypeStruct((T, v_cols), BF16),
            jax.ShapeDtypeStruct((T, c_cols), F32),
        ],
        compiler_params=_params(("parallel",)),
        name="in_proj",
    )(x2, g, w_perm, cos_t, sin_t)


def _attn_kernel(q_ref, k_ref, v_ref, lam_ref, g_ref, o_ref, m_sc, l_sc, acc_sc, *, tq, lambda_init):
    i = pl.program_id(2)
    q = q_ref[...]
    lane = lax.broadcasted_iota(jnp.int32, (1, LANES), 1)
    map0 = (lane // (ATTN_HEAD_DIM // 2)) % 2 == 0
    zero = jnp.zeros_like(q)
    qs = jnp.concatenate([jnp.where(map0, q, zero), jnp.where(map0, zero, q)], axis=0)

    m_sc[...] = jnp.full(m_sc.shape, NEG_INF, F32)
    l_sc[...] = jnp.zeros(l_sc.shape, F32)
    acc_sc[...] = jnp.zeros(acc_sc.shape, F32)

    def step(start, masked):
        k = k_ref[pl.ds(start, tq), :]
        v = v_ref[pl.ds(start, tq), :]
        s = lax.dot_general(qs, k, (((1,), (1,)), ((), ())), preferred_element_type=F32)
        if masked:
            row = lax.broadcasted_iota(jnp.int32, (2 * tq, tq), 0)
            col = lax.broadcasted_iota(jnp.int32, (2 * tq, tq), 1)
            s = jnp.where(col <= row % tq, s, NEG_INF)
        m_prev = m_sc[...]
        m_new = jnp.maximum(m_prev, jnp.max(s, axis=1, keepdims=True))
        alpha = jnp.exp(m_prev - m_new)
        p = jnp.exp(s - jnp.tile(m_new, (1, tq // LANES)))
        l_sc[...] = alpha * l_sc[...] + jnp.sum(p, axis=1, keepdims=True)
        acc_sc[...] = alpha * acc_sc[...] + jnp.dot(p.astype(BF16), v, preferred_element_type=F32)
        m_sc[...] = m_new

    def body(j, carry):
        step(pl.multiple_of(j * tq, tq), False)
        return carry

    lax.fori_loop(0, i, body, 0)
    step(pl.multiple_of(i * tq, tq), True)

    lv = lam_ref[...]
    lam = (jnp.exp(jnp.sum(lv[0:1] * lv[1:2], axis=1, keepdims=True))
           - jnp.exp(jnp.sum(lv[2:3] * lv[3:4], axis=1, keepdims=True)) + lambda_init)
    o = acc_sc[0:tq, :] / l_sc[0:tq, :] - lam * (acc_sc[tq:, :] / l_sc[tq:, :])
    o_ref[...] = (_rms(o, g_ref[...]) * (1.0 - lambda_init)).astype(o_ref.dtype)


def _diff_attn(qk, v, lamv, subln_g, batch, seq, tq, lambda_init):
    T = qk.shape[0]
    H = N_ATTN_HEADS
    nq = seq // tq
    return pl.pallas_call(
        functools.partial(_attn_kernel, tq=tq, lambda_init=lambda_init),
        grid=(batch, H, nq),
        in_specs=[
            pl.BlockSpec((tq, LANES), lambda b, h, i: (b * nq + i, h)),
            pl.BlockSpec((seq, LANES), lambda b, h, i: (b, H + h)),
            pl.BlockSpec((seq, LANES), lambda b, h, i: (b, h)),
            pl.BlockSpec((4, ATTN_HEAD_DIM), lambda b, h, i: (0, 0)),
            pl.BlockSpec((1, LANES), lambda b, h, i: (0, 0)),
        ],
        out_specs=pl.BlockSpec((tq, LANES), lambda b, h, i: (b * nq + i, h)),
        out_shape=jax.ShapeDtypeStruct((T, H * LANES), BF16),
        scratch_shapes=[
            pltpu.VMEM((2 * tq, LANES), F32),
            pltpu.VMEM((2 * tq, LANES), F32),
            pltpu.VMEM((2 * tq, LANES), F32),
        ],
        compiler_params=_params(("parallel", "parallel", "arbitrary")),
        name="diff_attn",
    )(qk, qk, v, lamv, subln_g)


def _conv_kernel(cur_ref, prev_ref, w_ref, b_ref, g_ref, beta_ref, o_ref, u_sc, *, ts, tiles_per_seq, width, rows):
    i = pl.program_id(0)
    first = (i % tiles_per_seq) == 0
    pc = prev_ref[...]
    up = pc[:, :width] * jax.nn.sigmoid(pc[:, width:])
    u_sc[0:CONV_HALO, :] = jnp.where(first, jnp.zeros_like(up), up)
    c = cur_ref[...]
    u_sc[CONV_HALO:, :] = c[:, :width] * jax.nn.sigmoid(c[:, width:])
    bias = b_ref[...]
    gam = g_ref[...]
    beta = beta_ref[...]
    shift = CONV_HALO - (CONV_KERNEL - 1)

    def chunk(r, carry):
        r0 = pl.multiple_of(r * rows, rows)
        acc = jnp.broadcast_to(bias, (rows, width))
        win = u_sc[pl.ds(r0, rows + CONV_HALO), :]
        for b in range(8):
            taps = [j for j in range(CONV_KERNEL) if (shift + j) % 8 == b]
            if not taps:
                continue
            hi = max(shift + j for j in taps) - b
            wb = win[b:b + hi + rows, :]
            for j in taps:
                a8 = shift + j - b
                acc = acc + w_ref[j:j + 1, :] * wb[a8:a8 + rows, :]
        mu = jnp.mean(acc, axis=-1, keepdims=True)
        d = acc - mu
        var = jnp.mean(d * d, axis=-1, keepdims=True)
        y = d * lax.rsqrt(var + EPS) * gam + beta
        o_ref[pl.ds(r0, rows), :] = (y * jax.nn.sigmoid(y)).astype(o_ref.dtype)
        return carry

    lax.fori_loop(0, ts // rows, chunk, 0)


def _conv_module(cvg, w, b, ln_g, ln_b, seq, ts):
    T, two_w = cvg.shape
    width = two_w // 2
    rows = 32
    hb = ts // CONV_HALO
    return pl.pallas_call(
        functools.partial(_conv_kernel, ts=ts, tiles_per_seq=seq // ts, width=width, rows=rows),
        grid=(T // ts,),
        in_specs=[
            pl.BlockSpec((ts, two_w), lambda i: (i, 0)),
            pl.BlockSpec((CONV_HALO, two_w), lambda i: (jnp.maximum(i * hb - 1, 0), 0)),
            pl.BlockSpec((CONV_KERNEL, width), lambda i: (0, 0)),
            pl.BlockSpec((1, width), lambda i: (0, 0)),
            pl.BlockSpec((1, width), lambda i: (0, 0)),
            pl.BlockSpec((1, width), lambda i: (0, 0)),
        ],
        out_specs=pl.BlockSpec((ts, width), lambda i: (i, 0)),
        out_shape=jax.ShapeDtypeStruct((T, width), BF16),
        scratch_shapes=[pltpu.VMEM((ts + CONV_HALO, width), F32)],
        compiler_params=_params(("parallel",)),
        name="conv_module",
    )(cvg, cvg, w, b, ln_g, ln_b)


def _top_rows(s, n_pick, payload=None):
    n = s.shape[0]
    iota = lax.broadcasted_iota(jnp.int32, s.shape, 0).astype(F32)
    vals, picks = [], []
    for _ in range(n_pick):
        mx = jnp.max(s, axis=0, keepdims=True)
        pos = jnp.min(jnp.where(s == mx, iota, float(n)), axis=0, keepdims=True)
        sel = iota == pos
        vals.append(mx)
        if payload is None:
            picks.append(pos.astype(jnp.int32))
        else:
            picks.append(jnp.sum(jnp.where(sel, payload, 0), axis=0, keepdims=True))
        s = jnp.where(sel, NEG_INF, s)
    return vals, picks


def _route_kernel(x_ref, at_ref, cv_ref, wo_ref, g_ref, wq_ref, keys_ref, h_ref, m_ref, e_ref, gt_ref, *, aw):
    h1 = (x_ref[...]
          + jnp.dot(at_ref[...], wo_ref[0:aw, :], preferred_element_type=F32)
          + jnp.dot(cv_ref[...], wo_ref[aw:, :], preferred_element_type=F32))
    h_ref[...] = h1
    m = _rms(h1, g_ref[...])
    m_ref[...] = m
    mb = m.astype(BF16)
    pairs = [(i, j) for i in range(PEER_TOPK) for j in range(PEER_TOPK) if (i + 1) * (j + 1) <= PEER_TOPK]
    n_pad = -len(pairs) % 8
    experts, gates = [], []
    for hd in range(N_PEER_HEADS):
        sv, si = [], []
        for c in range(2):
            c0 = (hd * 2 + c) * PEER_HALF
            q = jnp.dot(mb, wq_ref[:, c0:c0 + PEER_HALF], preferred_element_type=F32).astype(BF16)
            st = lax.dot_general(keys_ref[c], q, (((1,), (1,)), ((), ())), preferred_element_type=F32)
            v_, i_ = _top_rows(st, PEER_TOPK)
            sv.append(v_)
            si.append(i_)
        tm = sv[0][0].shape[1]
        cand = jnp.concatenate([sv[0][i] + sv[1][j] for i, j in pairs]
                               + [jnp.full((n_pad, tm), NEG_INF, F32)], axis=0)
        cexp = jnp.concatenate([si[0][i] * N_KEYS + si[1][j] for i, j in pairs]
                               + [jnp.zeros((n_pad, tm), jnp.int32)], axis=0)
        cs, ce = _top_rows(cand, PEER_TOPK, payload=cexp)
        ex = [jnp.exp(c_ - cs[0]) for c_ in cs]
        den = ex[0]
        for e_ in ex[1:]:
            den = den + e_
        experts += ce
        gates += [e_ / den for e_ in ex]
    e_ref[...] = jnp.concatenate(experts, axis=0).T
    gt_ref[...] = jnp.concatenate(gates, axis=0)


def _route(x2, attn, conv, w_out, g, wq, keys, tm):
    T, D = x2.shape
    aw = attn.shape[1]
    cw = conv.shape[1]
    qc = wq.shape[1]
    hk = N_PEER_HEADS * PEER_TOPK
    return pl.pallas_call(
        functools.partial(_route_kernel, aw=aw),
        grid=(T // tm,),
        in_specs=[
            pl.BlockSpec((tm, D), lambda i: (i, 0)),
            pl.BlockSpec((tm, aw), lambda i: (i, 0)),
            pl.BlockSpec((tm, cw), lambda i: (i, 0)),
            pl.BlockSpec((aw + cw, D), lambda i: (0, 0)),
            pl.BlockSpec((1, D), lambda i: (0, 0)),
            pl.BlockSpec((D, qc), lambda i: (0, 0)),
            pl.BlockSpec((2, N_KEYS, PEER_HALF), lambda i: (0, 0, 0)),
        ],
        out_specs=[
            pl.BlockSpec((tm, D), lambda i: (i, 0)),
            pl.BlockSpec((tm, D), lambda i: (i, 0)),
            pl.BlockSpec((tm, hk), lambda i: (i, 0)),
            pl.BlockSpec((hk, tm), lambda i: (0, i)),
        ],
        out_shape=[
            jax.ShapeDtypeStruct((T, D), F32),
            jax.ShapeDtypeStruct((T, D), F32),
            jax.ShapeDtypeStruct((T, hk), jnp.int32),
            jax.ShapeDtypeStruct((hk, T), F32),
        ],
        compiler_params=_params(("parallel",)),
        name="route",
    )(x2, attn, conv, w_out, g, wq, keys)


def _table_rows(tab):
    n, d = tab.shape
    return tab.astype(BF16).reshape(n, d // LANES, LANES)


def _peer_a_kernel(idx_ref, m_ref, gt_ref, tab_ref, h_ref, *scratch, tb, hk, sub):
    *p_parts, g_sc = scratch
    rows_per_part = hk // len(p_parts)
    lane = lax.broadcasted_iota(jnp.int32, (hk, tb), 1)

    def place(a_t, t):
        return jnp.where(lane == t, jnp.sum(g_sc[t], axis=1, keepdims=True), a_t)

    g_sc[0] = jnp.zeros(g_sc.shape[1:], F32)

    def token(t, a_t):
        a_t = place(a_t, jnp.maximum(t - 1, 0))
        x = m_ref[t]
        ids = idx_ref.at[t]
        for q, p_sc in enumerate(p_parts):
            for r in range(rows_per_part):
                p_sc[r * sub:(r + 1) * sub, :] = tab_ref[ids[q * rows_per_part + r]].astype(F32) * x
            g = p_sc[pl.ds(0, rows_per_part, stride=sub), :]
            for s in range(1, sub):
                g = g + p_sc[pl.ds(s, rows_per_part, stride=sub), :]
            g_sc[t, q * rows_per_part:(q + 1) * rows_per_part, :] = g
        return a_t

    a = place(lax.fori_loop(0, tb, token, jnp.zeros((hk, tb), F32)), tb - 1)
    h_ref[...] = (0.5 * a * (1.0 + lax.erf(a * (2.0 ** -0.5))) * gt_ref[...]).T


PEER_A_PARTS = 4


def _peer_a(idx, m3, gates, tab, tb, chunk, tc):
    hk = idx.shape[1]
    sub = tab.shape[1]
    b0 = chunk * (tc // tb)
    return pl.pallas_call(
        functools.partial(_peer_a_kernel, tb=tb, hk=hk, sub=sub),
        grid=(tc // tb,),
        in_specs=[
            pl.BlockSpec((tb, hk), lambda i: (b0 + i, 0), memory_space=pltpu.SMEM),
            pl.BlockSpec((tb, sub, LANES), lambda i: (b0 + i, 0, 0)),
            pl.BlockSpec((hk, tb), lambda i: (0, b0 + i)),
            pl.BlockSpec(tab.shape, lambda i: (0, 0, 0), pipeline_mode=pl.Buffered(1)),
        ],
        out_specs=pl.BlockSpec((tb, hk), lambda i: (i, 0)),
        out_shape=jax.ShapeDtypeStruct((tc, hk), F32),
        scratch_shapes=[pltpu.VMEM((hk // PEER_A_PARTS * sub, LANES), F32)] * PEER_A_PARTS
        + [pltpu.VMEM((tb, hk, LANES), F32)],
        compiler_params=_params(("arbitrary",)),
        name="peer_a",
    )(idx, m3, gates, tab)


SC_LANES = 16
SC_WORD_PAIR = 32
SC_ROW_BUFS = 4
SC_LOAD_GROUP = 4
SC_ROW_GROUP = 4


def _table_words(tab):
    n, d = tab.shape
    t = tab.astype(BF16).reshape(n, d // SC_WORD_PAIR, 2, SC_LANES).transpose(0, 1, 3, 2)
    return lax.bitcast_convert_type(t.reshape(n, d // 2, 2), jnp.int32)


def _peer_y_sc(idx, h, tab_words, chunk, tc):
    hk = idx.shape[1]
    dw = tab_words.shape[1]
    d = 2 * dw
    info = plsc.get_sparse_core_info()
    nc, ns = info.num_cores, info.num_subcores
    tpw = tc // (nc * ns)
    rq = hk // SC_ROW_BUFS
    n_vec = dw // SC_LANES
    assert tpw % 2 == 0 and n_vec % SC_LOAD_GROUP == 0 and rq % SC_ROW_GROUP == 0
    mesh = plsc.VectorSubcoreMesh(core_axis_name="c", subcore_axis_name="s")

    @functools.partial(
        pl.kernel, mesh=mesh,
        out_type=jax.ShapeDtypeStruct((tc, d), F32),
        scratch_types=[pltpu.VMEM((tpw, hk), jnp.int32), pltpu.VMEM((tpw, hk), F32)]
        + [pltpu.VMEM((rq, dw), jnp.int32)] * SC_ROW_BUFS
        + [pltpu.VMEM((d,), F32)] * 2
        + [pltpu.SemaphoreType.DMA] * (SC_ROW_BUFS + 2),
        compiler_params=pltpu.CompilerParams(needs_layout_passes=False),
        name="peer_y_sc",
    )
    def body(idx_hbm, h_hbm, tab_hbm, y_hbm, ids_v, w_v, *scratch):
        rows = scratch[:SC_ROW_BUFS]
        accs = scratch[SC_ROW_BUFS:SC_ROW_BUFS + 2]
        row_sems = scratch[SC_ROW_BUFS + 2:2 * SC_ROW_BUFS + 2]
        out_sems = scratch[2 * SC_ROW_BUFS + 2:]
        base = (lax.axis_index("s") * nc + lax.axis_index("c")) * tpw
        pltpu.sync_copy(idx_hbm.at[pl.ds(chunk * tc + base, tpw)], ids_v)
        pltpu.sync_copy(h_hbm.at[pl.ds(base, tpw)], w_v)

        def gather(i, q):
            return pltpu.make_async_copy(tab_hbm.at[ids_v.at[i, pl.ds(q * rq, rq)]], rows[q], row_sems[q])

        def put(i, par):
            return pltpu.make_async_copy(accs[par], y_hbm.at[base + i], out_sems[par])

        def accumulate(i, q, acc):
            lanes0 = jnp.zeros((SC_LANES,), jnp.int32)

            n_groups = n_vec // SC_LOAD_GROUP

            @pl.loop(0, rq, step=SC_ROW_GROUP)
            def _(r0):
                ws = [plsc.load_gather(w_v, [lanes0 + i, lanes0 + (q * rq + r0 + j)]) for j in range(SC_ROW_GROUP)]

                def load(g):
                    return [[rows[q][r0 + j, pl.ds((g * SC_LOAD_GROUP + c) * SC_LANES, SC_LANES)]
                             for c in range(SC_LOAD_GROUP)] for j in range(SC_ROW_GROUP)]

                words = load(0)
                for g in range(n_groups):
                    nxt = load(g + 1) if g + 1 < n_groups else None
                    for c in range(SC_LOAD_GROUP):
                        lo_sum = hi_sum = None
                        for j in range(SC_ROW_GROUP):
                            lo, hi = plsc.unpack(plsc.bitcast(words[j][c], BF16), format=plsc.PackFormat.INTERLEAVED,
                                                 preferred_element_type=F32)
                            lo_sum = ws[j] * lo if lo_sum is None else lo_sum + ws[j] * lo
                            hi_sum = ws[j] * hi if hi_sum is None else hi_sum + ws[j] * hi
                        col = (g * SC_LOAD_GROUP + c) * SC_WORD_PAIR
                        plsc.addupdate(acc.at[pl.ds(col, SC_LANES)], lo_sum)
                        plsc.addupdate(acc.at[pl.ds(col + SC_LANES, SC_LANES)], hi_sum)
                    words = nxt

        for q in range(SC_ROW_BUFS):
            gather(0, q).start()

        @pl.loop(0, tpw, step=2)
        def _(i0):
            for par in range(2):
                i = i0 + par

                @pl.when(i0 > 0)
                def _():
                    put(i - 2, par).wait()

                zero = jnp.zeros((SC_LANES,), F32)
                for c in range(d // SC_LANES):
                    accs[par][pl.ds(c * SC_LANES, SC_LANES)] = zero
                for q in range(SC_ROW_BUFS):
                    gather(i, q).wait()
                    accumulate(i, q, accs[par])

                    @pl.when(i + 1 < tpw)
                    def _():
                        gather(i + 1, q).start()

                put(i, par).start()

        for par in range(2):
            put(tpw - 2 + par, par).wait()

    return body(idx, h, tab_words)


def _final_kernel(h_ref, y_ref, p_ref, wp_ref, gp_ref, wg_ref, gf_ref, o_ref, *, apply_final):
    h2 = h_ref[...] + y_ref[...]
    e = jnp.dot(p_ref[...].astype(BF16), wp_ref[...], preferred_element_type=F32)
    n = _rms(h2, gp_ref[...]).astype(BF16)
    gate = jax.nn.sigmoid(jnp.dot(n, wg_ref[...], preferred_element_type=F32))
    h3 = h2 + e * gate
    o_ref[...] = _rms(h3, gf_ref[...]) if apply_final else h3


def _ple(h1, y, p2, w_proj, g_ple, w_gate, g_final, apply_final, tm):
    T, D = h1.shape
    pd = p2.shape[1]
    return pl.pallas_call(
        functools.partial(_final_kernel, apply_final=apply_final),
        grid=(T // tm,),
        in_specs=[
            pl.BlockSpec((tm, D), lambda i: (i, 0)),
            pl.BlockSpec((tm, D), lambda i: (i, 0)),
            pl.BlockSpec((tm, pd), lambda i: (i, 0)),
            pl.BlockSpec((pd, D), lambda i: (0, 0)),
            pl.BlockSpec((1, D), lambda i: (0, 0)),
            pl.BlockSpec((D, D), lambda i: (0, 0)),
            pl.BlockSpec((1, D), lambda i: (0, 0)),
        ],
        out_specs=pl.BlockSpec((tm, D), lambda i: (i, 0)),
        out_shape=jax.ShapeDtypeStruct((T, D), F32),
        compiler_params=_params(("parallel",)),
        name="ple_final",
    )(h1, y, p2, w_proj, g_ple, w_gate, g_final)


def _qk_permutation():
    half = ATTN_HEAD_DIM // 2
    perm = np.zeros(N_ATTN_HEADS * LANES, np.int32)
    for h in range(N_ATTN_HEADS):
        for c in range(2):
            for d in range(ATTN_HEAD_DIM):
                perm[h * LANES + (d // half) * 2 * half + c * half + d % half] = h * LANES + c * ATTN_HEAD_DIM + d
    return perm


def _rope_tables(seq):
    half = ATTN_HEAD_DIM // 2
    inv_freq = 1.0 / (ROPE_THETA ** (jnp.arange(half, dtype=F32) * 2.0 / ATTN_HEAD_DIM))
    ang = jnp.arange(seq).astype(F32)[:, None] * inv_freq[None, :]
    cos, sin = jnp.cos(ang), jnp.sin(ang)
    return jnp.tile(cos, (1, 4)), jnp.concatenate([-sin, -sin, sin, sin], axis=1)


def kernel(x, p, attn_norm_g, w_in, lambda_q1, lambda_k1, lambda_q2, lambda_k2, subln_g, conv_w, conv_b,
           conv_ln_g, conv_ln_b, w_out, ffn_norm_g, peer_wq, peer_keys, peer_u, peer_v, ple_norm_g,
           ple_w_gate, ple_w_proj, final_norm_g):
    B, S, D = x.shape
    T = B * S
    depth = w_in.shape[0]
    assert depth >= 1
    tm = min(512, S)
    tq = min(512, S)
    tr = min(256, S)
    tb = min(128, T)
    tc = min(4096, T)
    qk_half = N_ATTN_HEADS * 2 * ATTN_HEAD_DIM

    perm = _qk_permutation()
    col_order = np.concatenate([perm, qk_half + perm, np.arange(2 * qk_half, w_in.shape[2])])
    cos_t, sin_t = _rope_tables(S)
    row = lambda a: a.reshape(1, -1).astype(F32)

    h = x.reshape(T, D)
    for l in range(depth):
        lambda_init = 0.8 - 0.6 * math.exp(-0.3 * l)
        w_perm = w_in[l][:, col_order].astype(BF16)
        qk, v, cvg = _in_proj(h, row(attn_norm_g[l]), w_perm, cos_t, sin_t, S, tm)
        lamv = jnp.stack([lambda_q1[l], lambda_k1[l], lambda_q2[l], lambda_k2[l]]).astype(F32)
        attn = _diff_attn(qk, v, lamv, row(subln_g[l]), B, S, tq, lambda_init)
        conv = _conv_module(cvg, conv_w[l], row(conv_b[l]), row(conv_ln_g[l]), row(conv_ln_b[l]), S, tm)
        h1, m, experts, gates = _route(h, attn, conv, w_out[l].astype(BF16), row(ffn_norm_g[l]),
                                       peer_wq[l].astype(BF16), peer_keys[l].astype(BF16), tr)
        u_rows, v_words, m3 = _table_rows(peer_u[l]), _table_words(peer_v[l]), m.reshape(T, D // LANES, LANES)
        ys = []
        for c in range(T // tc):
            hw = _peer_a(experts, m3, gates, u_rows, tb, c, tc)
            ys.append(_peer_y_sc(experts, hw, v_words, c, tc))
        y = jnp.concatenate(ys, axis=0)
        h = _ple(h1, y, p[l].reshape(T, -1), ple_w_proj[l].astype(BF16), row(ple_norm_g[l]),
                 ple_w_gate[l].astype(BF16), row(final_norm_g), l == depth - 1, tm)
    return h.reshape(B, S, D)
```

```python
import functools
import math

import numpy as np
import jax
import jax.numpy as jnp
from jax import lax
from jax.experimental import pallas as pl
from jax.experimental.pallas import tpu as pltpu
from jax.experimental.pallas import tpu_sc as plsc

EPS = 1e-6
ROPE_THETA = 10000.0
ATTN_HEAD_DIM = 64
N_ATTN_HEADS = 4
CONV_KERNEL = 31
N_PEER_HEADS = 8
N_KEYS = 128
PEER_TOPK = 16
PEER_HALF = 128

LANES = 128
CONV_HALO = 32
VMEM_LIMIT = 56 * 1024 * 1024

F32 = jnp.float32
BF16 = jnp.bfloat16
NEG_INF = float("-inf")


def _rms(x, g):
    ms = jnp.mean(x * x, axis=-1, keepdims=True)
    return x * lax.rsqrt(ms + EPS) * g


def _params(sem):
    return pltpu.CompilerParams(dimension_semantics=sem, vmem_limit_bytes=VMEM_LIMIT)


def _inproj_kernel(x_ref, g_ref, w_ref, cos_ref, sin_ref, qk_ref, v_ref, c_ref, *, qk_cols, v_cols):
    a = _rms(x_ref[...], g_ref[...]).astype(BF16)
    cos = cos_ref[...]
    sin = sin_ref[...]
    for col in range(0, qk_cols, 2 * LANES):
        z = jnp.dot(a, w_ref[:, col:col + 2 * LANES], preferred_element_type=F32)
        for half in range(2):
            zz = z[:, half * LANES:(half + 1) * LANES]
            r = zz * cos + pltpu.roll(zz, LANES // 2, 1) * sin
            c0 = col + half * LANES
            if c0 < qk_cols // 2:
                r = r * (ATTN_HEAD_DIM ** -0.5)
            qk_ref[:, c0:c0 + LANES] = r.astype(BF16)
    v_ref[...] = jnp.dot(a, w_ref[:, qk_cols:qk_cols + v_cols], preferred_element_type=F32).astype(BF16)
    c_ref[...] = jnp.dot(a, w_ref[:, qk_cols + v_cols:], preferred_element_type=F32)


def _in_proj(x2, g, w_perm, cos_t, sin_t, seq, tm):
    T, D = x2.shape
    n_cols = w_perm.shape[1]
    qk_cols = 2 * N_ATTN_HEADS * 2 * ATTN_HEAD_DIM
    v_cols = N_ATTN_HEADS * 2 * ATTN_HEAD_DIM
    c_cols = n_cols - qk_cols - v_cols
    spt = seq // tm
    return pl.pallas_call(
        functools.partial(_inproj_kernel, qk_cols=qk_cols, v_cols=v_cols),
        grid=(T // tm,),
        in_specs=[
            pl.BlockSpec((tm, D), lambda i: (i, 0)),
            pl.BlockSpec((1, D), lambda i: (0, 0)),
            pl.BlockSpec((D, n_cols), lambda i: (0, 0)),
            pl.BlockSpec((tm, LANES), lambda i: (i % spt, 0)),
            pl.BlockSpec((tm, LANES), lambda i: (i % spt, 0)),
        ],
        out_specs=[
            pl.BlockSpec((tm, qk_cols), lambda i: (i, 0)),
            pl.BlockSpec((tm, v_cols), lambda i: (i, 0)),
            pl.BlockSpec((tm, c_cols), lambda i: (i, 0)),
        ],
        out_shape=[
            jax.ShapeDtypeStruct((T, qk_cols), BF16),
            jax.ShapeDtypeStruct((T, v_cols), BF16),
            jax.ShapeDtypeStruct((T, c_cols), F32),
        ],
        compiler_params=_params(("parallel",)),
        name="in_proj",
    )(x2, g, w_perm, cos_t, sin_t)


def _attn_kernel(q_ref, k_ref, v_ref, lam_ref, g_ref, o_ref, m_sc, l_sc, acc_sc, *, tq, lambda_init):
    i = pl.program_id(2)
    q = q_ref[...]
    lane = lax.broadcasted_iota(jnp.int32, (1, LANES), 1)
    map0 = (lane // (ATTN_HEAD_DIM // 2)) % 2 == 0
    zero = jnp.zeros_like(q)
    qs = jnp.concatenate([jnp.where(map0, q, zero), jnp.where(map0, zero, q)], axis=0)

    m_sc[...] = jnp.full(m_sc.shape, NEG_INF, F32)
    l_sc[...] = jnp.zeros(l_sc.shape, F32)
    acc_sc[...] = jnp.zeros(acc_sc.shape, F32)

    def step(start, masked):
        k = k_ref[pl.ds(start, tq), :]
        v = v_ref[pl.ds(start, tq), :]
        s = lax.dot_general(qs, k, (((1,), (1,)), ((), ())), preferred_element_type=F32)
        if masked:
            row = lax.broadcasted_iota(jnp.int32, (2 * tq, tq), 0)
            col = lax.broadcasted_iota(jnp.int32, (2 * tq, tq), 1)
            s = jnp.where(col <= row % tq, s, NEG_INF)
        m_prev = m_sc[...]
        m_new = jnp.maximum(m_prev, jnp.max(s, axis=1, keepdims=True))
        alpha = jnp.exp(m_prev - m_new)
        p = jnp.exp(s - jnp.tile(m_new, (1, tq // LANES)))
        l_sc[...] = alpha * l_sc[...] + jnp.sum(p, axis=1, keepdims=True)
        acc_sc[...] = alpha * acc_sc[...] + jnp.dot(p.astype(BF16), v, preferred_element_type=F32)
        m_sc[...] = m_new

    def body(j, carry):
        step(pl.multiple_of(j * tq, tq), False)
        return carry

    lax.fori_loop(0, i, body, 0)
    step(pl.multiple_of(i * tq, tq), True)

    lv = lam_ref[...]
    lam = (jnp.exp(jnp.sum(lv[0:1] * lv[1:2], axis=1, keepdims=True))
           - jnp.exp(jnp.sum(lv[2:3] * lv[3:4], axis=1, keepdims=True)) + lambda_init)
    o = acc_sc[0:tq, :] / l_sc[0:tq, :] - lam * (acc_sc[tq:, :] / l_sc[tq:, :])
    o_ref[...] = (_rms(o, g_ref[...]) * (1.0 - lambda_init)).astype(o_ref.dtype)


def _diff_attn(qk, v, lamv, subln_g, batch, seq, tq, lambda_init):
    T = qk.shape[0]
    H = N_ATTN_HEADS
    nq = seq // tq
    return pl.pallas_call(
        functools.partial(_attn_kernel, tq=tq, lambda_init=lambda_init),
        grid=(batch, H, nq),
        in_specs=[
            pl.BlockSpec((tq, LANES), lambda b, h, i: (b * nq + i, h)),
            pl.BlockSpec((seq, LANES), lambda b, h, i: (b, H + h)),
            pl.BlockSpec((seq, LANES), lambda b, h, i: (b, h)),
            pl.BlockSpec((4, ATTN_HEAD_DIM), lambda b, h, i: (0, 0)),
            pl.BlockSpec((1, LANES), lambda b, h, i: (0, 0)),
        ],
        out_specs=pl.BlockSpec((tq, LANES), lambda b, h, i: (b * nq + i, h)),
        out_shape=jax.ShapeDtypeStruct((T, H * LANES), BF16),
        scratch_shapes=[
            pltpu.VMEM((2 * tq, LANES), F32),
            pltpu.VMEM((2 * tq, LANES), F32),
            pltpu.VMEM((2 * tq, LANES), F32),
        ],
        compiler_params=_params(("parallel", "parallel", "arbitrary")),
        name="diff_attn",
    )(qk, qk, v, lamv, subln_g)


def _conv_kernel(cur_ref, prev_ref, w_ref, b_ref, g_ref, beta_ref, o_ref, u_sc, *, ts, tiles_per_seq, width, rows):
    i = pl.program_id(0)
    first = (i % tiles_per_seq) == 0
    pc = prev_ref[...]
    up = pc[:, :width] * jax.nn.sigmoid(pc[:, width:])
    u_sc[0:CONV_HALO, :] = jnp.where(first, jnp.zeros_like(up), up)
    c = cur_ref[...]
    u_sc[CONV_HALO:, :] = c[:, :width] * jax.nn.sigmoid(c[:, width:])
    bias = b_ref[...]
    gam = g_ref[...]
    beta = beta_ref[...]
    shift = CONV_HALO - (CONV_KERNEL - 1)

    def chunk(r, carry):
        r0 = pl.multiple_of(r * rows, rows)
        acc = jnp.broadcast_to(bias, (rows, width))
        win = u_sc[pl.ds(r0, rows + CONV_HALO), :]
        for b in range(8):
            taps = [j for j in range(CONV_KERNEL) if (shift + j) % 8 == b]
            if not taps:
                continue
            hi = max(shift + j for j in taps) - b
            wb = win[b:b + hi + rows, :]
            for j in taps:
                a8 = shift + j - b
                acc = acc + w_ref[j:j + 1, :] * wb[a8:a8 + rows, :]
        mu = jnp.mean(acc, axis=-1, keepdims=True)
        d = acc - mu
        var = jnp.mean(d * d, axis=-1, keepdims=True)
        y = d * lax.rsqrt(var + EPS) * gam + beta
        o_ref[pl.ds(r0, rows), :] = (y * jax.nn.sigmoid(y)).astype(o_ref.dtype)
        return carry

    lax.fori_loop(0, ts // rows, chunk, 0)


def _conv_module(cvg, w, b, ln_g, ln_b, seq, ts):
    T, two_w = cvg.shape
    width = two_w // 2
    rows = 32
    hb = ts // CONV_HALO
    return pl.pallas_call(
        functools.partial(_conv_kernel, ts=ts, tiles_per_seq=seq // ts, width=width, rows=rows),
        grid=(T // ts,),
        in_specs=[
            pl.BlockSpec((ts, two_w), lambda i: (i, 0)),
            pl.BlockSpec((CONV_HALO, two_w), lambda i: (jnp.maximum(i * hb - 1, 0), 0)),
            pl.BlockSpec((CONV_KERNEL, width), lambda i: (0, 0)),
            pl.BlockSpec((1, width), lambda i: (0, 0)),
            pl.BlockSpec((1, width), lambda i: (0, 0)),
            pl.BlockSpec((1, width), lambda i: (0, 0)),
        ],
        out_specs=pl.BlockSpec((ts, width), lambda i: (i, 0)),
        out_shape=jax.ShapeDtypeStruct((T, width), BF16),
        scratch_shapes=[pltpu.VMEM((ts + CONV_HALO, width), F32)],
        compiler_params=_params(("parallel",)),
        name="conv_module",
    )(cvg, cvg, w, b, ln_g, ln_b)


def _top_rows(s, n_pick, payload=None):
    n = s.shape[0]
    iota = lax.broadcasted_iota(jnp.int32, s.shape, 0).astype(F32)
    vals, picks = [], []
    for _ in range(n_pick):
        mx = jnp.max(s, axis=0, keepdims=True)
        pos = jnp.min(jnp.where(s == mx, iota, float(n)), axis=0, keepdims=True)
        sel = iota == pos
        vals.append(mx)
        if payload is None:
            picks.append(pos.astype(jnp.int32))
        else:
            picks.append(jnp.sum(jnp.where(sel, payload, 0), axis=0, keepdims=True))
        s = jnp.where(sel, NEG_INF, s)
    return vals, picks


def _route_kernel(x_ref, at_ref, cv_ref, wo_ref, g_ref, wq_ref, keys_ref, h_ref, m_ref, e_ref, gt_ref, *, aw):
    h1 = (x_ref[...]
          + jnp.dot(at_ref[...], wo_ref[0:aw, :], preferred_element_type=F32)
          + jnp.dot(cv_ref[...], wo_ref[aw:, :], preferred_element_type=F32))
    h_ref[...] = h1
    m = _rms(h1, g_ref[...])
    m_ref[...] = m
    mb = m.astype(BF16)
    pairs = [(i, j) for i in range(PEER_TOPK) for j in range(PEER_TOPK) if (i + 1) * (j + 1) <= PEER_TOPK]
    n_pad = -len(pairs) % 8
    experts, gates = [], []
    for hd in range(N_PEER_HEADS):
        sv, si = [], []
        for c in range(2):
            c0 = (hd * 2 + c) * PEER_HALF
            q = jnp.dot(mb, wq_ref[:, c0:c0 + PEER_HALF], preferred_element_type=F32).astype(BF16)
            st = lax.dot_general(keys_ref[c], q, (((1,), (1,)), ((), ())), preferred_element_type=F32)
            v_, i_ = _top_rows(st, PEER_TOPK)
            sv.append(v_)
            si.append(i_)
        tm = sv[0][0].shape[1]
        cand = jnp.concatenate([sv[0][i] + sv[1][j] for i, j in pairs]
                               + [jnp.full((n_pad, tm), NEG_INF, F32)], axis=0)
        cexp = jnp.concatenate([si[0][i] * N_KEYS + si[1][j] for i, j in pairs]
                               + [jnp.zeros((n_pad, tm), jnp.int32)], axis=0)
        cs, ce = _top_rows(cand, PEER_TOPK, payload=cexp)
        ex = [jnp.exp(c_ - cs[0]) for c_ in cs]
        den = ex[0]
        for e_ in ex[1:]:
            den = den + e_
        experts += ce
        gates += [e_ / den for e_ in ex]
    e_ref[...] = jnp.concatenate(experts, axis=0).T
    gt_ref[...] = jnp.concatenate(gates, axis=0)


def _route(x2, attn, conv, w_out, g, wq, keys, tm):
    T, D = x2.shape
    aw = attn.shape[1]
    cw = conv.shape[1]
    qc = wq.shape[1]
    hk = N_PEER_HEADS * PEER_TOPK
    return pl.pallas_call(
        functools.partial(_route_kernel, aw=aw),
        grid=(T // tm,),
        in_specs=[
            pl.BlockSpec((tm, D), lambda i: (i, 0)),
            pl.BlockSpec((tm, aw), lambda i: (i, 0)),
            pl.BlockSpec((tm, cw), lambda i: (i, 0)),
            pl.BlockSpec((aw + cw, D), lambda i: (0, 0)),
            pl.BlockSpec((1, D), lambda i: (0, 0)),
            pl.BlockSpec((D, qc), lambda i: (0, 0)),
            pl.BlockSpec((2, N_KEYS, PEER_HALF), lambda i: (0, 0, 0)),
        ],
        out_specs=[
            pl.BlockSpec((tm, D), lambda i: (i, 0)),
            pl.BlockSpec((tm, D), lambda i: (i, 0)),
            pl.BlockSpec((tm, hk), lambda i: (i, 0)),
            pl.BlockSpec((hk, tm), lambda i: (0, i)),
        ],
        out_shape=[
            jax.ShapeDtypeStruct((T, D), F32),
            jax.ShapeDtypeStruct((T, D), F32),
            jax.ShapeDtypeStruct((T, hk), jnp.int32),
            jax.ShapeDtypeStruct((hk, T), F32),
        ],
        compiler_params=_params(("parallel",)),
        name="route",
    )(x2, attn, conv, w_out, g, wq, keys)


def _table_rows(tab):
    n, d = tab.shape
    return tab.astype(BF16).reshape(n, d // LANES, LANES)


def _peer_a_kernel(idx_ref, m_ref, gt_ref, tab_ref, h_ref, *scratch, tb, hk, sub):
    *p_parts, g_sc = scratch
    rows_per_part = hk // len(p_parts)
    lane = lax.broadcasted_iota(jnp.int32, (hk, tb), 1)

    def place(a_t, t):
        return jnp.where(lane == t, jnp.sum(g_sc[t], axis=1, keepdims=True), a_t)

    g_sc[0] = jnp.zeros(g_sc.shape[1:], F32)

    def token(t, a_t):
        a_t = place(a_t, jnp.maximum(t - 1, 0))
        x = m_ref[t]
        ids = idx_ref.at[t]
        def sublane_sums(q):
            p_sc = p_parts[q]
            g = p_sc[pl.ds(0, rows_per_part, stride=sub), :]
            for s in range(1, sub):
                g = g + p_sc[pl.ds(s, rows_per_part, stride=sub), :]
            g_sc[t, q * rows_per_part:(q + 1) * rows_per_part, :] = g

        for q, p_sc in enumerate(p_parts):
            for r in range(rows_per_part):
                p_sc[r * sub:(r + 1) * sub, :] = tab_ref[ids[q * rows_per_part + r]].astype(F32) * x
            if q > 0:
                sublane_sums(q - 1)
        sublane_sums(len(p_parts) - 1)
        return a_t

    a = place(lax.fori_loop(0, tb, token, jnp.zeros((hk, tb), F32)), tb - 1)
    h_ref[...] = (0.5 * a * (1.0 + lax.erf(a * (2.0 ** -0.5))) * gt_ref[...]).T


PEER_A_PARTS = 4


def _peer_a(idx, m3, gates, tab, tb, start, tc):
    hk = idx.shape[1]
    sub = tab.shape[1]
    b0 = start // tb
    return pl.pallas_call(
        functools.partial(_peer_a_kernel, tb=tb, hk=hk, sub=sub),
        grid=(tc // tb,),
        in_specs=[
            pl.BlockSpec((tb, hk), lambda i: (b0 + i, 0), memory_space=pltpu.SMEM),
            pl.BlockSpec((tb, sub, LANES), lambda i: (b0 + i, 0, 0)),
            pl.BlockSpec((hk, tb), lambda i: (0, b0 + i)),
            pl.BlockSpec(tab.shape, lambda i: (0, 0, 0), pipeline_mode=pl.Buffered(1)),
        ],
        out_specs=pl.BlockSpec((tb, hk), lambda i: (i, 0)),
        out_shape=jax.ShapeDtypeStruct((tc, hk), F32),
        scratch_shapes=[pltpu.VMEM((hk // PEER_A_PARTS * sub, LANES), F32)] * PEER_A_PARTS
        + [pltpu.VMEM((tb, hk, LANES), F32)],
        compiler_params=_params(("arbitrary",)),
        name="peer_a",
    )(idx, m3, gates, tab)


SC_LANES = 16
SC_WORD_PAIR = 32
SC_ROW_BUFS = 4
SC_LOAD_GROUP = 4
SC_ROW_GROUP = 4


def _table_words(tab):
    n, d = tab.shape
    t = tab.astype(BF16).reshape(n, d // SC_WORD_PAIR, 2, SC_LANES).transpose(0, 1, 3, 2)
    return lax.bitcast_convert_type(t.reshape(n, d // 2, 2), jnp.int32)


def _peer_y_sc(idx, h, tab_words, start, tc):
    hk = idx.shape[1]
    dw = tab_words.shape[1]
    d = 2 * dw
    info = plsc.get_sparse_core_info()
    nc, ns = info.num_cores, info.num_subcores
    tpw = tc // (nc * ns)
    rq = hk // SC_ROW_BUFS
    n_vec = dw // SC_LANES
    assert tpw % 2 == 0 and n_vec % SC_LOAD_GROUP == 0 and rq % SC_ROW_GROUP == 0
    mesh = plsc.VectorSubcoreMesh(core_axis_name="c", subcore_axis_name="s")

    @functools.partial(
        pl.kernel, mesh=mesh,
        out_type=jax.ShapeDtypeStruct((tc, d), F32),
        scratch_types=[pltpu.VMEM((tpw, hk), jnp.int32), pltpu.VMEM((tpw, hk), F32)]
        + [pltpu.VMEM((rq, dw), jnp.int32)] * SC_ROW_BUFS
        + [pltpu.VMEM((d,), F32)] * 2
        + [pltpu.SemaphoreType.DMA] * (SC_ROW_BUFS + 2),
        compiler_params=pltpu.CompilerParams(needs_layout_passes=False),
        name="peer_y_sc",
    )
    def body(idx_hbm, h_hbm, tab_hbm, y_hbm, ids_v, w_v, *scratch):
        rows = scratch[:SC_ROW_BUFS]
        accs = scratch[SC_ROW_BUFS:SC_ROW_BUFS + 2]
        row_sems = scratch[SC_ROW_BUFS + 2:2 * SC_ROW_BUFS + 2]
        out_sems = scratch[2 * SC_ROW_BUFS + 2:]
        base = (lax.axis_index("s") * nc + lax.axis_index("c")) * tpw
        pltpu.sync_copy(idx_hbm.at[pl.ds(start + base, tpw)], ids_v)
        pltpu.sync_copy(h_hbm.at[pl.ds(base, tpw)], w_v)

        def gather(i, q):
            return pltpu.make_async_copy(tab_hbm.at[ids_v.at[i, pl.ds(q * rq, rq)]], rows[q], row_sems[q])

        def put(i, par):
            return pltpu.make_async_copy(accs[par], y_hbm.at[base + i], out_sems[par])

        def accumulate(i, q, acc):
            lanes0 = jnp.zeros((SC_LANES,), jnp.int32)

            n_groups = n_vec // SC_LOAD_GROUP

            @pl.loop(0, rq, step=SC_ROW_GROUP)
            def _(r0):
                ws = [plsc.load_gather(w_v, [lanes0 + i, lanes0 + (q * rq + r0 + j)]) for j in range(SC_ROW_GROUP)]

                def load(g):
                    return [[rows[q][r0 + j, pl.ds((g * SC_LOAD_GROUP + c) * SC_LANES, SC_LANES)]
                             for c in range(SC_LOAD_GROUP)] for j in range(SC_ROW_GROUP)]

                words = load(0)
                for g in range(n_groups):
                    nxt = load(g + 1) if g + 1 < n_groups else None
                    for c in range(SC_LOAD_GROUP):
                        lo_sum = hi_sum = None
                        for j in range(SC_ROW_GROUP):
                            lo, hi = plsc.unpack(plsc.bitcast(words[j][c], BF16), format=plsc.PackFormat.INTERLEAVED,
                                                 preferred_element_type=F32)
                            lo_sum = ws[j] * lo if lo_sum is None else lo_sum + ws[j] * lo
                            hi_sum = ws[j] * hi if hi_sum is None else hi_sum + ws[j] * hi
                        col = (g * SC_LOAD_GROUP + c) * SC_WORD_PAIR
                        plsc.addupdate(acc.at[pl.ds(col, SC_LANES)], lo_sum)
                        plsc.addupdate(acc.at[pl.ds(col + SC_LANES, SC_LANES)], hi_sum)
                    words = nxt

        for q in range(SC_ROW_BUFS):
            gather(0, q).start()

        @pl.loop(0, tpw, step=2)
        def _(i0):
            for par in range(2):
                i = i0 + par

                @pl.when(i0 > 0)
                def _():
                    put(i - 2, par).wait()

                zero = jnp.zeros((SC_LANES,), F32)
                for c in range(d // SC_LANES):
                    accs[par][pl.ds(c * SC_LANES, SC_LANES)] = zero
                for q in range(SC_ROW_BUFS):
                    gather(i, q).wait()
                    accumulate(i, q, accs[par])

                    @pl.when(i + 1 < tpw)
                    def _():
                        gather(i + 1, q).start()

                put(i, par).start()

        for par in range(2):
            put(tpw - 2 + par, par).wait()

    return body(idx, h, tab_words)


def _final_kernel(h_ref, y_ref, p_ref, wp_ref, gp_ref, wg_ref, gf_ref, o_ref, *, apply_final):
    h2 = h_ref[...] + y_ref[...]
    e = jnp.dot(p_ref[...].astype(BF16), wp_ref[...], preferred_element_type=F32)
    n = _rms(h2, gp_ref[...]).astype(BF16)
    gate = jax.nn.sigmoid(jnp.dot(n, wg_ref[...], preferred_element_type=F32))
    h3 = h2 + e * gate
    o_ref[...] = _rms(h3, gf_ref[...]) if apply_final else h3


def _ple(h1, y, p2, w_proj, g_ple, w_gate, g_final, apply_final, tm):
    T, D = h1.shape
    pd = p2.shape[1]
    return pl.pallas_call(
        functools.partial(_final_kernel, apply_final=apply_final),
        grid=(T // tm,),
        in_specs=[
            pl.BlockSpec((tm, D), lambda i: (i, 0)),
            pl.BlockSpec((tm, D), lambda i: (i, 0)),
            pl.BlockSpec((tm, pd), lambda i: (i, 0)),
            pl.BlockSpec((pd, D), lambda i: (0, 0)),
            pl.BlockSpec((1, D), lambda i: (0, 0)),
            pl.BlockSpec((D, D), lambda i: (0, 0)),
            pl.BlockSpec((1, D), lambda i: (0, 0)),
        ],
        out_specs=pl.BlockSpec((tm, D), lambda i: (i, 0)),
        out_shape=jax.ShapeDtypeStruct((T, D), F32),
        compiler_params=_params(("parallel",)),
        name="ple_final",
    )(h1, y, p2, w_proj, g_ple, w_gate, g_final)


def _qk_permutation():
    half = ATTN_HEAD_DIM // 2
    perm = np.zeros(N_ATTN_HEADS * LANES, np.int32)
    for h in range(N_ATTN_HEADS):
        for c in range(2):
            for d in range(ATTN_HEAD_DIM):
                perm[h * LANES + (d // half) * 2 * half + c * half + d % half] = h * LANES + c * ATTN_HEAD_DIM + d
    return perm


def _rope_tables(seq):
    half = ATTN_HEAD_DIM // 2
    inv_freq = 1.0 / (ROPE_THETA ** (jnp.arange(half, dtype=F32) * 2.0 / ATTN_HEAD_DIM))
    ang = jnp.arange(seq).astype(F32)[:, None] * inv_freq[None, :]
    cos, sin = jnp.cos(ang), jnp.sin(ang)
    return jnp.tile(cos, (1, 4)), jnp.concatenate([-sin, -sin, sin, sin], axis=1)


def _token_chunks(total, big):
    sizes = [big] * (total // big)
    if big % 1024 == 0:
        sizes[-1:] = [big // 2, big // 4, big // 4]
    return sizes


def kernel(x, p, attn_norm_g, w_in, lambda_q1, lambda_k1, lambda_q2, lambda_k2, subln_g, conv_w, conv_b,
           conv_ln_g, conv_ln_b, w_out, ffn_norm_g, peer_wq, peer_keys, peer_u, peer_v, ple_norm_g,
           ple_w_gate, ple_w_proj, final_norm_g):
    B, S, D = x.shape
    T = B * S
    depth = w_in.shape[0]
    assert depth >= 1
    tm = min(512, S)
    tq = min(512, S)
    tr = min(256, S)
    tb = min(128, T)
    tc = min(4096, T)
    qk_half = N_ATTN_HEADS * 2 * ATTN_HEAD_DIM

    perm = _qk_permutation()
    col_order = np.concatenate([perm, qk_half + perm, np.arange(2 * qk_half, w_in.shape[2])])
    cos_t, sin_t = _rope_tables(S)
    row = lambda a: a.reshape(1, -1).astype(F32)

    h = x.reshape(T, D)
    for l in range(depth):
        lambda_init = 0.8 - 0.6 * math.exp(-0.3 * l)
        w_perm = w_in[l][:, col_order].astype(BF16)
        qk, v, cvg = _in_proj(h, row(attn_norm_g[l]), w_perm, cos_t, sin_t, S, tm)
        lamv = jnp.stack([lambda_q1[l], lambda_k1[l], lambda_q2[l], lambda_k2[l]]).astype(F32)
        attn = _diff_attn(qk, v, lamv, row(subln_g[l]), B, S, tq, lambda_init)
        conv = _conv_module(cvg, conv_w[l], row(conv_b[l]), row(conv_ln_g[l]), row(conv_ln_b[l]), S, tm)
        h1, m, experts, gates = _route(h, attn, conv, w_out[l].astype(BF16), row(ffn_norm_g[l]),
                                       peer_wq[l].astype(BF16), peer_keys[l].astype(BF16), tr)
        u_rows, v_words, m3 = _table_rows(peer_u[l]), _table_words(peer_v[l]), m.reshape(T, D // LANES, LANES)
        ys, start = [], 0
        for size in _token_chunks(T, tc):
            hw = _peer_a(experts, m3, gates, u_rows, tb, start, size)
            ys.append(_peer_y_sc(experts, hw, v_words, start, size))
            start += size
        y = jnp.concatenate(ys, axis=0)
        h = _ple(h1, y, p[l].reshape(T, -1), ple_w_proj[l].astype(BF16), row(ple_norm_g[l]),
                 ple_w_gate[l].astype(BF16), row(final_norm_g), l == depth - 1, tm)
    return h.reshape(B, S, D)
```

```python
import functools
import math

import numpy as np
import jax
import jax.numpy as jnp
from jax import lax
from jax.experimental import pallas as pl
from jax.experimental.pallas import tpu as pltpu
from jax.experimental.pallas import tpu_sc as plsc

EPS = 1e-6
ROPE_THETA = 10000.0
ATTN_HEAD_DIM = 64
N_ATTN_HEADS = 4
CONV_KERNEL = 31
N_PEER_HEADS = 8
N_KEYS = 128
PEER_TOPK = 16
PEER_HALF = 128

LANES = 128
CONV_HALO = 32
VMEM_LIMIT = 56 * 1024 * 1024

F32 = jnp.float32
BF16 = jnp.bfloat16
NEG_INF = float("-inf")


def _rms(x, g):
    ms = jnp.mean(x * x, axis=-1, keepdims=True)
    return x * lax.rsqrt(ms + EPS) * g


def _params(sem):
    return pltpu.CompilerParams(dimension_semantics=sem, vmem_limit_bytes=VMEM_LIMIT)


def _inproj_kernel(x_ref, g_ref, w_ref, cos_ref, sin_ref, qk_ref, v_ref, c_ref, *, qk_cols, v_cols):
    a = _rms(x_ref[...], g_ref[...]).astype(BF16)
    cos = cos_ref[...]
    sin = sin_ref[...]
    for col in range(0, qk_cols, 2 * LANES):
        z = jnp.dot(a, w_ref[:, col:col + 2 * LANES], preferred_element_type=F32)
        for half in range(2):
            zz = z[:, half * LANES:(half + 1) * LANES]
            r = zz * cos + pltpu.roll(zz, LANES // 2, 1) * sin
            c0 = col + half * LANES
            if c0 < qk_cols // 2:
                r = r * (ATTN_HEAD_DIM ** -0.5)
            qk_ref[:, c0:c0 + LANES] = r.astype(BF16)
    v_ref[...] = jnp.dot(a, w_ref[:, qk_cols:qk_cols + v_cols], preferred_element_type=F32).astype(BF16)
    c_ref[...] = jnp.dot(a, w_ref[:, qk_cols + v_cols:], preferred_element_type=F32)


def _in_proj(x2, g, w_perm, cos_t, sin_t, seq, tm):
    T, D = x2.shape
    n_cols = w_perm.shape[1]
    qk_cols = 2 * N_ATTN_HEADS * 2 * ATTN_HEAD_DIM
    v_cols = N_ATTN_HEADS * 2 * ATTN_HEAD_DIM
    c_cols = n_cols - qk_cols - v_cols
    spt = seq // tm
    return pl.pallas_call(
        functools.partial(_inproj_kernel, qk_cols=qk_cols, v_cols=v_cols),
        grid=(T // tm,),
        in_specs=[
            pl.BlockSpec((tm, D), lambda i: (i, 0)),
            pl.BlockSpec((1, D), lambda i: (0, 0)),
            pl.BlockSpec((D, n_cols), lambda i: (0, 0)),
            pl.BlockSpec((tm, LANES), lambda i: (i % spt, 0)),
            pl.BlockSpec((tm, LANES), lambda i: (i % spt, 0)),
        ],
        out_specs=[
            pl.BlockSpec((tm, qk_cols), lambda i: (i, 0)),
            pl.BlockSpec((tm, v_cols), lambda i: (i, 0)),
            pl.BlockSpec((tm, c_cols), lambda i: (i, 0)),
        ],
        out_shape=[
            jax.ShapeDtypeStruct((T, qk_cols), BF16),
            jax.ShapeDtypeStruct((T, v_cols), BF16),
            jax.ShapeDtypeStruct((T, c_cols), F32),
        ],
        compiler_params=_params(("parallel",)),
        name="in_proj",
    )(x2, g, w_perm, cos_t, sin_t)


def _attn_kernel(q_ref, k_ref, v_ref, lam_ref, g_ref, o_ref, m_sc, l_sc, acc_sc, *, tq, lambda_init):
    i = pl.program_id(2)
    q = q_ref[...]
    lane = lax.broadcasted_iota(jnp.int32, (1, LANES), 1)
    map0 = (lane // (ATTN_HEAD_DIM // 2)) % 2 == 0
    zero = jnp.zeros_like(q)
    qs = jnp.concatenate([jnp.where(map0, q, zero), jnp.where(map0, zero, q)], axis=0)

    m_sc[...] = jnp.full(m_sc.shape, NEG_INF, F32)
    l_sc[...] = jnp.zeros(l_sc.shape, F32)
    acc_sc[...] = jnp.zeros(acc_sc.shape, F32)

    def step(start, masked):
        k = k_ref[pl.ds(start, tq), :]
        v = v_ref[pl.ds(start, tq), :]
        s = lax.dot_general(qs, k, (((1,), (1,)), ((), ())), preferred_element_type=F32)
        if masked:
            row = lax.broadcasted_iota(jnp.int32, (2 * tq, tq), 0)
            col = lax.broadcasted_iota(jnp.int32, (2 * tq, tq), 1)
            s = jnp.where(col <= row % tq, s, NEG_INF)
        m_prev = m_sc[...]
        m_new = jnp.maximum(m_prev, jnp.max(s, axis=1, keepdims=True))
        alpha = jnp.exp(m_prev - m_new)
        p = jnp.exp(s - jnp.tile(m_new, (1, tq // LANES)))
        l_sc[...] = alpha * l_sc[...] + jnp.sum(p, axis=1, keepdims=True)
        acc_sc[...] = alpha * acc_sc[...] + jnp.dot(p.astype(BF16), v, preferred_element_type=F32)
        m_sc[...] = m_new

    def body(j, carry):
        step(pl.multiple_of(j * tq, tq), False)
        return carry

    lax.fori_loop(0, i, body, 0)
    step(pl.multiple_of(i * tq, tq), True)

    lv = lam_ref[...]
    lam = (jnp.exp(jnp.sum(lv[0:1] * lv[1:2], axis=1, keepdims=True))
           - jnp.exp(jnp.sum(lv[2:3] * lv[3:4], axis=1, keepdims=True)) + lambda_init)
    o = acc_sc[0:tq, :] / l_sc[0:tq, :] - lam * (acc_sc[tq:, :] / l_sc[tq:, :])
    o_ref[...] = (_rms(o, g_ref[...]) * (1.0 - lambda_init)).astype(o_ref.dtype)


def _diff_attn(qk, v, lamv, subln_g, batch, seq, tq, lambda_init):
    T = qk.shape[0]
    H = N_ATTN_HEADS
    nq = seq // tq
    return pl.pallas_call(
        functools.partial(_attn_kernel, tq=tq, lambda_init=lambda_init),
        grid=(batch, H, nq),
        in_specs=[
            pl.BlockSpec((tq, LANES), lambda b, h, i: (b * nq + i, h)),
            pl.BlockSpec((seq, LANES), lambda b, h, i: (b, H + h)),
            pl.BlockSpec((seq, LANES), lambda b, h, i: (b, h)),
            pl.BlockSpec((4, ATTN_HEAD_DIM), lambda b, h, i: (0, 0)),
            pl.BlockSpec((1, LANES), lambda b, h, i: (0, 0)),
        ],
        out_specs=pl.BlockSpec((tq, LANES), lambda b, h, i: (b * nq + i, h)),
        out_shape=jax.ShapeDtypeStruct((T, H * LANES), BF16),
        scratch_shapes=[
            pltpu.VMEM((2 * tq, LANES), F32),
            pltpu.VMEM((2 * tq, LANES), F32),
            pltpu.VMEM((2 * tq, LANES), F32),
        ],
        compiler_params=_params(("parallel", "parallel", "arbitrary")),
        name="diff_attn",
    )(qk, qk, v, lamv, subln_g)


def _conv_kernel(cur_ref, prev_ref, w_ref, b_ref, g_ref, beta_ref, o_ref, u_sc, *, ts, tiles_per_seq, width, rows):
    i = pl.program_id(0)
    first = (i % tiles_per_seq) == 0
    pc = prev_ref[...]
    up = pc[:, :width] * jax.nn.sigmoid(pc[:, width:])
    u_sc[0:CONV_HALO, :] = jnp.where(first, jnp.zeros_like(up), up)
    c = cur_ref[...]
    u_sc[CONV_HALO:, :] = c[:, :width] * jax.nn.sigmoid(c[:, width:])
    bias = b_ref[...]
    gam = g_ref[...]
    beta = beta_ref[...]
    shift = CONV_HALO - (CONV_KERNEL - 1)

    def chunk(r, carry):
        r0 = pl.multiple_of(r * rows, rows)
        acc = jnp.broadcast_to(bias, (rows, width))
        win = u_sc[pl.ds(r0, rows + CONV_HALO), :]
        for b in range(8):
            taps = [j for j in range(CONV_KERNEL) if (shift + j) % 8 == b]
            if not taps:
                continue
            hi = max(shift + j for j in taps) - b
            wb = win[b:b + hi + rows, :]
            for j in taps:
                a8 = shift + j - b
                acc = acc + w_ref[j:j + 1, :] * wb[a8:a8 + rows, :]
        mu = jnp.mean(acc, axis=-1, keepdims=True)
        d = acc - mu
        var = jnp.mean(d * d, axis=-1, keepdims=True)
        y = d * lax.rsqrt(var + EPS) * gam + beta
        o_ref[pl.ds(r0, rows), :] = (y * jax.nn.sigmoid(y)).astype(o_ref.dtype)
        return carry

    lax.fori_loop(0, ts // rows, chunk, 0)


def _conv_module(cvg, w, b, ln_g, ln_b, seq, ts):
    T, two_w = cvg.shape
    width = two_w // 2
    rows = 32
    hb = ts // CONV_HALO
    return pl.pallas_call(
        functools.partial(_conv_kernel, ts=ts, tiles_per_seq=seq // ts, width=width, rows=rows),
        grid=(T // ts,),
        in_specs=[
            pl.BlockSpec((ts, two_w), lambda i: (i, 0)),
            pl.BlockSpec((CONV_HALO, two_w), lambda i: (jnp.maximum(i * hb - 1, 0), 0)),
            pl.BlockSpec((CONV_KERNEL, width), lambda i: (0, 0)),
            pl.BlockSpec((1, width), lambda i: (0, 0)),
            pl.BlockSpec((1, width), lambda i: (0, 0)),
            pl.BlockSpec((1, width), lambda i: (0, 0)),
        ],
        out_specs=pl.BlockSpec((ts, width), lambda i: (i, 0)),
        out_shape=jax.ShapeDtypeStruct((T, width), BF16),
        scratch_shapes=[pltpu.VMEM((ts + CONV_HALO, width), F32)],
        compiler_params=_params(("parallel",)),
        name="conv_module",
    )(cvg, cvg, w, b, ln_g, ln_b)


def _top_rows(s, n_pick, payload=None):
    n = s.shape[0]
    iota = lax.broadcasted_iota(jnp.int32, s.shape, 0).astype(F32)
    vals, picks = [], []
    for _ in range(n_pick):
        mx = jnp.max(s, axis=0, keepdims=True)
        pos = jnp.min(jnp.where(s == mx, iota, float(n)), axis=0, keepdims=True)
        sel = iota == pos
        vals.append(mx)
        if payload is None:
            picks.append(pos.astype(jnp.int32))
        else:
            picks.append(jnp.sum(jnp.where(sel, payload, 0), axis=0, keepdims=True))
        s = jnp.where(sel, NEG_INF, s)
    return vals, picks


def _route_kernel(x_ref, at_ref, cv_ref, wo_ref, g_ref, wq_ref, keys_ref, h_ref, m_ref, e_ref, gt_ref, *, aw):
    h1 = (x_ref[...]
          + jnp.dot(at_ref[...], wo_ref[0:aw, :], preferred_element_type=F32)
          + jnp.dot(cv_ref[...], wo_ref[aw:, :], preferred_element_type=F32))
    h_ref[...] = h1
    m = _rms(h1, g_ref[...])
    m_ref[...] = m
    mb = m.astype(BF16)
    pairs = [(i, j) for i in range(PEER_TOPK) for j in range(PEER_TOPK) if (i + 1) * (j + 1) <= PEER_TOPK]
    n_pad = -len(pairs) % 8
    experts, gates = [], []
    for hd in range(N_PEER_HEADS):
        sv, si = [], []
        for c in range(2):
            c0 = (hd * 2 + c) * PEER_HALF
            q = jnp.dot(mb, wq_ref[:, c0:c0 + PEER_HALF], preferred_element_type=F32).astype(BF16)
            st = lax.dot_general(keys_ref[c], q, (((1,), (1,)), ((), ())), preferred_element_type=F32)
            v_, i_ = _top_rows(st, PEER_TOPK)
            sv.append(v_)
            si.append(i_)
        tm = sv[0][0].shape[1]
        cand = jnp.concatenate([sv[0][i] + sv[1][j] for i, j in pairs]
                               + [jnp.full((n_pad, tm), NEG_INF, F32)], axis=0)
        cexp = jnp.concatenate([si[0][i] * N_KEYS + si[1][j] for i, j in pairs]
                               + [jnp.zeros((n_pad, tm), jnp.int32)], axis=0)
        cs, ce = _top_rows(cand, PEER_TOPK, payload=cexp)
        ex = [jnp.exp(c_ - cs[0]) for c_ in cs]
        den = ex[0]
        for e_ in ex[1:]:
            den = den + e_
        experts += ce
        gates += [e_ / den for e_ in ex]
    e_ref[...] = jnp.concatenate(experts, axis=0).T
    gt_ref[...] = jnp.concatenate(gates, axis=0)


def _route(x2, attn, conv, w_out, g, wq, keys, tm):
    T, D = x2.shape
    aw = attn.shape[1]
    cw = conv.shape[1]
    qc = wq.shape[1]
    hk = N_PEER_HEADS * PEER_TOPK
    return pl.pallas_call(
        functools.partial(_route_kernel, aw=aw),
        grid=(T // tm,),
        in_specs=[
            pl.BlockSpec((tm, D), lambda i: (i, 0)),
            pl.BlockSpec((tm, aw), lambda i: (i, 0)),
            pl.BlockSpec((tm, cw), lambda i: (i, 0)),
            pl.BlockSpec((aw + cw, D), lambda i: (0, 0)),
            pl.BlockSpec((1, D), lambda i: (0, 0)),
            pl.BlockSpec((D, qc), lambda i: (0, 0)),
            pl.BlockSpec((2, N_KEYS, PEER_HALF), lambda i: (0, 0, 0)),
        ],
        out_specs=[
            pl.BlockSpec((tm, D), lambda i: (i, 0)),
            pl.BlockSpec((tm, D), lambda i: (i, 0)),
            pl.BlockSpec((tm, hk), lambda i: (i, 0)),
            pl.BlockSpec((hk, tm), lambda i: (0, i)),
        ],
        out_shape=[
            jax.ShapeDtypeStruct((T, D), F32),
            jax.ShapeDtypeStruct((T, D), F32),
            jax.ShapeDtypeStruct((T, hk), jnp.int32),
            jax.ShapeDtypeStruct((hk, T), F32),
        ],
        compiler_params=_params(("parallel",)),
        name="route",
    )(x2, attn, conv, w_out, g, wq, keys)


def _table_rows(tab):
    n, d = tab.shape
    return tab.astype(BF16).reshape(n, d // LANES, LANES)


def _peer_a_kernel(idx_ref, m_ref, gt_ref, tab_ref, h_ref, *scratch, tb, hk, sub):
    *p_parts, g_sc = scratch
    rows_per_part = hk // len(p_parts)
    lane = lax.broadcasted_iota(jnp.int32, (hk, tb), 1)

    def place(a_t, t):
        return jnp.where(lane == t, jnp.sum(g_sc[t], axis=1, keepdims=True), a_t)

    g_sc[0] = jnp.zeros(g_sc.shape[1:], F32)

    def token(t, a_t):
        a_t = place(a_t, jnp.maximum(t - 1, 0))
        x = m_ref[t]
        ids = idx_ref.at[t]
        def sublane_sums(q):
            p_sc = p_parts[q]
            g = p_sc[pl.ds(0, rows_per_part, stride=sub), :]
            for s in range(1, sub):
                g = g + p_sc[pl.ds(s, rows_per_part, stride=sub), :]
            g_sc[t, q * rows_per_part:(q + 1) * rows_per_part, :] = g

        for q, p_sc in enumerate(p_parts):
            for r in range(rows_per_part):
                p_sc[r * sub:(r + 1) * sub, :] = tab_ref[ids[q * rows_per_part + r]].astype(F32) * x
            if q > 0:
                sublane_sums(q - 1)
        sublane_sums(len(p_parts) - 1)
        return a_t

    a = place(lax.fori_loop(0, tb, token, jnp.zeros((hk, tb), F32)), tb - 1)
    h_ref[...] = (0.5 * a * (1.0 + lax.erf(a * (2.0 ** -0.5))) * gt_ref[...]).T


PEER_A_PARTS = 4


def _peer_a(idx, m3, gates, tab, tb, start, tc):
    hk = idx.shape[1]
    sub = tab.shape[1]
    b0 = start // tb
    return pl.pallas_call(
        functools.partial(_peer_a_kernel, tb=tb, hk=hk, sub=sub),
        grid=(tc // tb,),
        in_specs=[
            pl.BlockSpec((tb, hk), lambda i: (b0 + i, 0), memory_space=pltpu.SMEM),
            pl.BlockSpec((tb, sub, LANES), lambda i: (b0 + i, 0, 0)),
            pl.BlockSpec((hk, tb), lambda i: (0, b0 + i)),
            pl.BlockSpec(tab.shape, lambda i: (0, 0, 0), pipeline_mode=pl.Buffered(1)),
        ],
        out_specs=pl.BlockSpec((tb, hk), lambda i: (i, 0)),
        out_shape=jax.ShapeDtypeStruct((tc, hk), F32),
        scratch_shapes=[pltpu.VMEM((hk // PEER_A_PARTS * sub, LANES), F32)] * PEER_A_PARTS
        + [pltpu.VMEM((tb, hk, LANES), F32)],
        compiler_params=_params(("arbitrary",)),
        name="peer_a",
    )(idx, m3, gates, tab)


SC_LANES = 16
SC_WORD_PAIR = 32
SC_ROW_BUFS = 4
SC_LOAD_GROUP = 4
SC_ROW_GROUP = 4


def _table_words(tab):
    n, d = tab.shape
    t = tab.astype(BF16).reshape(n, d // SC_WORD_PAIR, 2, SC_LANES).transpose(0, 1, 3, 2)
    return lax.bitcast_convert_type(t.reshape(n, d // 2, 2), jnp.int32)


def _peer_y_sc(idx, h, tab_words, start, tc):
    hk = idx.shape[1]
    dw = tab_words.shape[1]
    d = 2 * dw
    info = plsc.get_sparse_core_info()
    nc, ns = info.num_cores, info.num_subcores
    tpw = tc // (nc * ns)
    rq = hk // SC_ROW_BUFS
    n_vec = dw // SC_LANES
    assert tpw % 2 == 0 and n_vec % SC_LOAD_GROUP == 0 and rq % SC_ROW_GROUP == 0
    mesh = plsc.VectorSubcoreMesh(core_axis_name="c", subcore_axis_name="s")

    @functools.partial(
        pl.kernel, mesh=mesh,
        out_type=jax.ShapeDtypeStruct((tc, d), F32),
        scratch_types=[pltpu.VMEM((tpw, hk), jnp.int32), pltpu.VMEM((tpw, hk), F32)]
        + [pltpu.VMEM((rq, dw), jnp.int32)] * SC_ROW_BUFS
        + [pltpu.VMEM((d,), F32)] * 2
        + [pltpu.SemaphoreType.DMA] * (SC_ROW_BUFS + 2),
        compiler_params=pltpu.CompilerParams(needs_layout_passes=False),
        name="peer_y_sc",
    )
    def body(idx_hbm, h_hbm, tab_hbm, y_hbm, ids_v, w_v, *scratch):
        rows = scratch[:SC_ROW_BUFS]
        accs = scratch[SC_ROW_BUFS:SC_ROW_BUFS + 2]
        row_sems = scratch[SC_ROW_BUFS + 2:2 * SC_ROW_BUFS + 2]
        out_sems = scratch[2 * SC_ROW_BUFS + 2:]
        base = (lax.axis_index("s") * nc + lax.axis_index("c")) * tpw
        pltpu.sync_copy(idx_hbm.at[pl.ds(start + base, tpw)], ids_v)
        pltpu.sync_copy(h_hbm.at[pl.ds(base, tpw)], w_v)

        def gather(i, q):
            return pltpu.make_async_copy(tab_hbm.at[ids_v.at[i, pl.ds(q * rq, rq)]], rows[q], row_sems[q])

        def put(i, par):
            return pltpu.make_async_copy(accs[par], y_hbm.at[base + i], out_sems[par])

        def accumulate(i, q, acc):
            lanes0 = jnp.zeros((SC_LANES,), jnp.int32)

            n_groups = n_vec // SC_LOAD_GROUP

            @pl.loop(0, rq, step=SC_ROW_GROUP)
            def _(r0):
                ws = [plsc.load_gather(w_v, [lanes0 + i, lanes0 + (q * rq + r0 + j)]) for j in range(SC_ROW_GROUP)]

                def load(g):
                    return [[rows[q][r0 + j, pl.ds((g * SC_LOAD_GROUP + c) * SC_LANES, SC_LANES)]
                             for c in range(SC_LOAD_GROUP)] for j in range(SC_ROW_GROUP)]

                words = load(0)
                for g in range(n_groups):
                    nxt = load(g + 1) if g + 1 < n_groups else None
                    for c in range(SC_LOAD_GROUP):
                        lo_sum = hi_sum = None
                        for j in range(SC_ROW_GROUP):
                            lo, hi = plsc.unpack(plsc.bitcast(words[j][c], BF16), format=plsc.PackFormat.INTERLEAVED,
                                                 preferred_element_type=F32)
                            lo_sum = ws[j] * lo if lo_sum is None else lo_sum + ws[j] * lo
                            hi_sum = ws[j] * hi if hi_sum is None else hi_sum + ws[j] * hi
                        col = (g * SC_LOAD_GROUP + c) * SC_WORD_PAIR
                        plsc.addupdate(acc.at[pl.ds(col, SC_LANES)], lo_sum)
                        plsc.addupdate(acc.at[pl.ds(col + SC_LANES, SC_LANES)], hi_sum)
                    words = nxt

        for q in range(SC_ROW_BUFS):
            gather(0, q).start()

        @pl.loop(0, tpw, step=2)
        def _(i0):
            for par in range(2):
                i = i0 + par

                @pl.when(i0 > 0)
                def _():
                    put(i - 2, par).wait()

                zero = jnp.zeros((SC_LANES,), F32)
                for c in range(d // SC_LANES):
                    accs[par][pl.ds(c * SC_LANES, SC_LANES)] = zero
                for q in range(SC_ROW_BUFS):
                    gather(i, q).wait()
                    accumulate(i, q, accs[par])

                    @pl.when(i + 1 < tpw)
                    def _():
                        gather(i + 1, q).start()

                put(i, par).start()

        for par in range(2):
            put(tpw - 2 + par, par).wait()

    return body(idx, h, tab_words)


def _final_kernel(h_ref, y_ref, p_ref, wp_ref, gp_ref, wg_ref, gf_ref, *rest, apply_final):
    o_ref = rest[-1]
    h2 = h_ref[...] + y_ref[...]
    e = jnp.dot(p_ref[...].astype(BF16), wp_ref[...], preferred_element_type=F32)
    n = _rms(h2, gp_ref[...]).astype(BF16)
    gate = jax.nn.sigmoid(jnp.dot(n, wg_ref[...], preferred_element_type=F32))
    h3 = h2 + e * gate
    o_ref[...] = _rms(h3, gf_ref[...]) if apply_final else h3


def _ple(h1, y_chunk, p2, w_proj, g_ple, w_gate, g_final, apply_final, tm, start, out_prev):
    T, D = h1.shape
    size = y_chunk.shape[0]
    pd = p2.shape[1]
    b0 = start // tm
    in_specs = [
        pl.BlockSpec((tm, D), lambda i: (b0 + i, 0)),
        pl.BlockSpec((tm, D), lambda i: (i, 0)),
        pl.BlockSpec((tm, pd), lambda i: (b0 + i, 0)),
        pl.BlockSpec((pd, D), lambda i: (0, 0)),
        pl.BlockSpec((1, D), lambda i: (0, 0)),
        pl.BlockSpec((D, D), lambda i: (0, 0)),
        pl.BlockSpec((1, D), lambda i: (0, 0)),
    ]
    args = [h1, y_chunk, p2, w_proj, g_ple, w_gate, g_final]
    aliases = {}
    if out_prev is not None:
        in_specs.append(pl.BlockSpec(memory_space=pl.ANY))
        args.append(out_prev)
        aliases = {len(args) - 1: 0}
    return pl.pallas_call(
        functools.partial(_final_kernel, apply_final=apply_final),
        grid=(size // tm,),
        in_specs=in_specs,
        out_specs=pl.BlockSpec((tm, D), lambda i: (b0 + i, 0)),
        out_shape=jax.ShapeDtypeStruct((T, D), F32),
        input_output_aliases=aliases,
        compiler_params=_params(("parallel",)),
        name="ple_final",
    )(*args)


def _qk_permutation():
    half = ATTN_HEAD_DIM // 2
    perm = np.zeros(N_ATTN_HEADS * LANES, np.int32)
    for h in range(N_ATTN_HEADS):
        for c in range(2):
            for d in range(ATTN_HEAD_DIM):
                perm[h * LANES + (d // half) * 2 * half + c * half + d % half] = h * LANES + c * ATTN_HEAD_DIM + d
    return perm


def _rope_tables(seq):
    half = ATTN_HEAD_DIM // 2
    inv_freq = 1.0 / (ROPE_THETA ** (jnp.arange(half, dtype=F32) * 2.0 / ATTN_HEAD_DIM))
    ang = jnp.arange(seq).astype(F32)[:, None] * inv_freq[None, :]
    cos, sin = jnp.cos(ang), jnp.sin(ang)
    return jnp.tile(cos, (1, 4)), jnp.concatenate([-sin, -sin, sin, sin], axis=1)


def _token_chunks(total, big):
    sizes = [big] * (total // big)
    if big % 1024 == 0:
        sizes[-1:] = [big // 2, big // 4, big // 4]
    return sizes


def kernel(x, p, attn_norm_g, w_in, lambda_q1, lambda_k1, lambda_q2, lambda_k2, subln_g, conv_w, conv_b,
           conv_ln_g, conv_ln_b, w_out, ffn_norm_g, peer_wq, peer_keys, peer_u, peer_v, ple_norm_g,
           ple_w_gate, ple_w_proj, final_norm_g):
    B, S, D = x.shape
    T = B * S
    depth = w_in.shape[0]
    assert depth >= 1
    tm = min(512, S)
    tq = min(512, S)
    tr = min(256, S)
    tb = min(128, T)
    tc = min(4096, T)
    qk_half = N_ATTN_HEADS * 2 * ATTN_HEAD_DIM

    perm = _qk_permutation()
    col_order = np.concatenate([perm, qk_half + perm, np.arange(2 * qk_half, w_in.shape[2])])
    cos_t, sin_t = _rope_tables(S)
    row = lambda a: a.reshape(1, -1).astype(F32)

    h = x.reshape(T, D)
    for l in range(depth):
        lambda_init = 0.8 - 0.6 * math.exp(-0.3 * l)
        w_perm = w_in[l][:, col_order].astype(BF16)
        qk, v, cvg = _in_proj(h, row(attn_norm_g[l]), w_perm, cos_t, sin_t, S, tm)
        lamv = jnp.stack([lambda_q1[l], lambda_k1[l], lambda_q2[l], lambda_k2[l]]).astype(F32)
        attn = _diff_attn(qk, v, lamv, row(subln_g[l]), B, S, tq, lambda_init)
        conv = _conv_module(cvg, conv_w[l], row(conv_b[l]), row(conv_ln_g[l]), row(conv_ln_b[l]), S, tm)
        h1, m, experts, gates = _route(h, attn, conv, w_out[l].astype(BF16), row(ffn_norm_g[l]),
                                       peer_wq[l].astype(BF16), peer_keys[l].astype(BF16), tr)
        u_rows, v_words, m3 = _table_rows(peer_u[l]), _table_words(peer_v[l]), m.reshape(T, D // LANES, LANES)
        ple_args = (p[l].reshape(T, -1), ple_w_proj[l].astype(BF16), row(ple_norm_g[l]), ple_w_gate[l].astype(BF16),
                    row(final_norm_g), l == depth - 1, tm)
        ys, start = [], 0
        for size in _token_chunks(T, tc):
            hw = _peer_a(experts, m3, gates, u_rows, tb, start, size)
            ys.append((start, _peer_y_sc(experts, hw, v_words, start, size)))
            start += size
        h = None
        for start, y_chunk in ys:
            h = _ple(h1, y_chunk, *ple_args, start, h)
    return h.reshape(B, S, D)
```

```python
import functools
import math

import numpy as np
import jax
import jax.numpy as jnp
from jax import lax
from jax.experimental import pallas as pl
from jax.experimental.pallas import tpu as pltpu
from jax.experimental.pallas import tpu_sc as plsc

EPS = 1e-6
ROPE_THETA = 10000.0
ATTN_HEAD_DIM = 64
N_ATTN_HEADS = 4
CONV_KERNEL = 31
N_PEER_HEADS = 8
N_KEYS = 128
PEER_TOPK = 16
PEER_HALF = 128

LANES = 128
CONV_HALO = 32
VMEM_LIMIT = 56 * 1024 * 1024

F32 = jnp.float32
BF16 = jnp.bfloat16
NEG_INF = float("-inf")


def _rms(x, g):
    ms = jnp.mean(x * x, axis=-1, keepdims=True)
    return x * lax.rsqrt(ms + EPS) * g


def _params(sem):
    return pltpu.CompilerParams(dimension_semantics=sem, vmem_limit_bytes=VMEM_LIMIT)


def _inproj_kernel(x_ref, g_ref, w_ref, cos_ref, sin_ref, qk_ref, v_ref, c_ref, *, qk_cols, v_cols):
    a = _rms(x_ref[...], g_ref[...]).astype(BF16)
    cos = cos_ref[...]
    sin = sin_ref[...]
    for col in range(0, qk_cols, 2 * LANES):
        z = jnp.dot(a, w_ref[:, col:col + 2 * LANES], preferred_element_type=F32)
        for half in range(2):
            zz = z[:, half * LANES:(half + 1) * LANES]
            r = zz * cos + pltpu.roll(zz, LANES // 2, 1) * sin
            c0 = col + half * LANES
            if c0 < qk_cols // 2:
                r = r * (ATTN_HEAD_DIM ** -0.5)
            qk_ref[:, c0:c0 + LANES] = r.astype(BF16)
    v_ref[...] = jnp.dot(a, w_ref[:, qk_cols:qk_cols + v_cols], preferred_element_type=F32).astype(BF16)
    c_ref[...] = jnp.dot(a, w_ref[:, qk_cols + v_cols:], preferred_element_type=F32)


def _in_proj(x2, g, w_perm, cos_t, sin_t, seq, tm):
    T, D = x2.shape
    n_cols = w_perm.shape[1]
    qk_cols = 2 * N_ATTN_HEADS * 2 * ATTN_HEAD_DIM
    v_cols = N_ATTN_HEADS * 2 * ATTN_HEAD_DIM
    c_cols = n_cols - qk_cols - v_cols
    spt = seq // tm
    return pl.pallas_call(
        functools.partial(_inproj_kernel, qk_cols=qk_cols, v_cols=v_cols),
        grid=(T // tm,),
        in_specs=[
            pl.BlockSpec((tm, D), lambda i: (i, 0)),
            pl.BlockSpec((1, D), lambda i: (0, 0)),
            pl.BlockSpec((D, n_cols), lambda i: (0, 0)),
            pl.BlockSpec((tm, LANES), lambda i: (i % spt, 0)),
            pl.BlockSpec((tm, LANES), lambda i: (i % spt, 0)),
        ],
        out_specs=[
            pl.BlockSpec((tm, qk_cols), lambda i: (i, 0)),
            pl.BlockSpec((tm, v_cols), lambda i: (i, 0)),
            pl.BlockSpec((tm, c_cols), lambda i: (i, 0)),
        ],
        out_shape=[
            jax.ShapeDtypeStruct((T, qk_cols), BF16),
            jax.ShapeDtypeStruct((T, v_cols), BF16),
            jax.ShapeDtypeStruct((T, c_cols), F32),
        ],
        compiler_params=_params(("parallel",)),
        name="in_proj",
    )(x2, g, w_perm, cos_t, sin_t)


def _attn_kernel(q_ref, k_ref, v_ref, lam_ref, g_ref, o_ref, m_sc, l_sc, acc_sc, *, tq, lambda_init):
    i = pl.program_id(2)
    q = q_ref[...]
    lane = lax.broadcasted_iota(jnp.int32, (1, LANES), 1)
    map0 = (lane // (ATTN_HEAD_DIM // 2)) % 2 == 0
    zero = jnp.zeros_like(q)
    qs = jnp.concatenate([jnp.where(map0, q, zero), jnp.where(map0, zero, q)], axis=0)

    m_sc[...] = jnp.full(m_sc.shape, NEG_INF, F32)
    l_sc[...] = jnp.zeros(l_sc.shape, F32)
    acc_sc[...] = jnp.zeros(acc_sc.shape, F32)

    def step(start, masked):
        k = k_ref[pl.ds(start, tq), :]
        v = v_ref[pl.ds(start, tq), :]
        s = lax.dot_general(qs, k, (((1,), (1,)), ((), ())), preferred_element_type=F32)
        if masked:
            row = lax.broadcasted_iota(jnp.int32, (2 * tq, tq), 0)
            col = lax.broadcasted_iota(jnp.int32, (2 * tq, tq), 1)
            s = jnp.where(col <= row % tq, s, NEG_INF)
        m_prev = m_sc[...]
        m_new = jnp.maximum(m_prev, jnp.max(s, axis=1, keepdims=True))
        alpha = jnp.exp(m_prev - m_new)
        p = jnp.exp(s - jnp.tile(m_new, (1, tq // LANES)))
        l_sc[...] = alpha * l_sc[...] + jnp.sum(p, axis=1, keepdims=True)
        acc_sc[...] = alpha * acc_sc[...] + jnp.dot(p.astype(BF16), v, preferred_element_type=F32)
        m_sc[...] = m_new

    def body(j, carry):
        step(pl.multiple_of(j * tq, tq), False)
        return carry

    lax.fori_loop(0, i, body, 0)
    step(pl.multiple_of(i * tq, tq), True)

    lv = lam_ref[...]
    lam = (jnp.exp(jnp.sum(lv[0:1] * lv[1:2], axis=1, keepdims=True))
           - jnp.exp(jnp.sum(lv[2:3] * lv[3:4], axis=1, keepdims=True)) + lambda_init)
    o = acc_sc[0:tq, :] / l_sc[0:tq, :] - lam * (acc_sc[tq:, :] / l_sc[tq:, :])
    o_ref[...] = (_rms(o, g_ref[...]) * (1.0 - lambda_init)).astype(o_ref.dtype)


def _diff_attn(qk, v, lamv, subln_g, batch, seq, tq, lambda_init):
    T = qk.shape[0]
    H = N_ATTN_HEADS
    nq = seq // tq
    return pl.pallas_call(
        functools.partial(_attn_kernel, tq=tq, lambda_init=lambda_init),
        grid=(batch, H, nq),
        in_specs=[
            pl.BlockSpec((tq, LANES), lambda b, h, i: (b * nq + i, h)),
            pl.BlockSpec((seq, LANES), lambda b, h, i: (b, H + h)),
            pl.BlockSpec((seq, LANES), lambda b, h, i: (b, h)),
            pl.BlockSpec((4, ATTN_HEAD_DIM), lambda b, h, i: (0, 0)),
            pl.BlockSpec((1, LANES), lambda b, h, i: (0, 0)),
        ],
        out_specs=pl.BlockSpec((tq, LANES), lambda b, h, i: (b * nq + i, h)),
        out_shape=jax.ShapeDtypeStruct((T, H * LANES), BF16),
        scratch_shapes=[
            pltpu.VMEM((2 * tq, LANES), F32),
            pltpu.VMEM((2 * tq, LANES), F32),
            pltpu.VMEM((2 * tq, LANES), F32),
        ],
        compiler_params=_params(("parallel", "parallel", "arbitrary")),
        name="diff_attn",
    )(qk, qk, v, lamv, subln_g)


def _conv_kernel(cur_ref, prev_ref, w_ref, b_ref, g_ref, beta_ref, o_ref, u_sc, *, ts, tiles_per_seq, width, rows):
    i = pl.program_id(0)
    first = (i % tiles_per_seq) == 0
    pc = prev_ref[...]
    up = pc[:, :width] * jax.nn.sigmoid(pc[:, width:])
    u_sc[0:CONV_HALO, :] = jnp.where(first, jnp.zeros_like(up), up)
    c = cur_ref[...]
    u_sc[CONV_HALO:, :] = c[:, :width] * jax.nn.sigmoid(c[:, width:])
    bias = b_ref[...]
    gam = g_ref[...]
    beta = beta_ref[...]
    shift = CONV_HALO - (CONV_KERNEL - 1)

    def chunk(r, carry):
        r0 = pl.multiple_of(r * rows, rows)
        acc = jnp.broadcast_to(bias, (rows, width))
        win = u_sc[pl.ds(r0, rows + CONV_HALO), :]
        for b in range(8):
            taps = [j for j in range(CONV_KERNEL) if (shift + j) % 8 == b]
            if not taps:
                continue
            hi = max(shift + j for j in taps) - b
            wb = win[b:b + hi + rows, :]
            for j in taps:
                a8 = shift + j - b
                acc = acc + w_ref[j:j + 1, :] * wb[a8:a8 + rows, :]
        mu = jnp.mean(acc, axis=-1, keepdims=True)
        d = acc - mu
        var = jnp.mean(d * d, axis=-1, keepdims=True)
        y = d * lax.rsqrt(var + EPS) * gam + beta
        o_ref[pl.ds(r0, rows), :] = (y * jax.nn.sigmoid(y)).astype(o_ref.dtype)
        return carry

    lax.fori_loop(0, ts // rows, chunk, 0)


def _conv_module(cvg, w, b, ln_g, ln_b, seq, ts):
    T, two_w = cvg.shape
    width = two_w // 2
    rows = 32
    hb = ts // CONV_HALO
    return pl.pallas_call(
        functools.partial(_conv_kernel, ts=ts, tiles_per_seq=seq // ts, width=width, rows=rows),
        grid=(T // ts,),
        in_specs=[
            pl.BlockSpec((ts, two_w), lambda i: (i, 0)),
            pl.BlockSpec((CONV_HALO, two_w), lambda i: (jnp.maximum(i * hb - 1, 0), 0)),
            pl.BlockSpec((CONV_KERNEL, width), lambda i: (0, 0)),
            pl.BlockSpec((1, width), lambda i: (0, 0)),
            pl.BlockSpec((1, width), lambda i: (0, 0)),
            pl.BlockSpec((1, width), lambda i: (0, 0)),
        ],
        out_specs=pl.BlockSpec((ts, width), lambda i: (i, 0)),
        out_shape=jax.ShapeDtypeStruct((T, width), BF16),
        scratch_shapes=[pltpu.VMEM((ts + CONV_HALO, width), F32)],
        compiler_params=_params(("parallel",)),
        name="conv_module",
    )(cvg, cvg, w, b, ln_g, ln_b)


def _top_rows(s, n_pick, payload=None):
    n = s.shape[0]
    iota = lax.broadcasted_iota(jnp.int32, s.shape, 0).astype(F32)
    vals, picks = [], []
    for _ in range(n_pick):
        mx = jnp.max(s, axis=0, keepdims=True)
        pos = jnp.min(jnp.where(s == mx, iota, float(n)), axis=0, keepdims=True)
        sel = iota == pos
        vals.append(mx)
        if payload is None:
            picks.append(pos.astype(jnp.int32))
        else:
            picks.append(jnp.sum(jnp.where(sel, payload, 0), axis=0, keepdims=True))
        s = jnp.where(sel, NEG_INF, s)
    return vals, picks


def _route_kernel(x_ref, at_ref, cv_ref, wo_ref, g_ref, wq_ref, keys_ref, h_ref, m_ref, e_ref, gt_ref, *, aw):
    h1 = (x_ref[...]
          + jnp.dot(at_ref[...], wo_ref[0:aw, :], preferred_element_type=F32)
          + jnp.dot(cv_ref[...], wo_ref[aw:, :], preferred_element_type=F32))
    h_ref[...] = h1
    m = _rms(h1, g_ref[...])
    m_ref[...] = m
    mb = m.astype(BF16)
    pairs = [(i, j) for i in range(PEER_TOPK) for j in range(PEER_TOPK) if (i + 1) * (j + 1) <= PEER_TOPK]
    n_pad = -len(pairs) % 8
    experts, gates = [], []
    for hd in range(N_PEER_HEADS):
        sv, si = [], []
        for c in range(2):
            c0 = (hd * 2 + c) * PEER_HALF
            q = jnp.dot(mb, wq_ref[:, c0:c0 + PEER_HALF], preferred_element_type=F32).astype(BF16)
            st = lax.dot_general(keys_ref[c], q, (((1,), (1,)), ((), ())), preferred_element_type=F32)
            v_, i_ = _top_rows(st, PEER_TOPK)
            sv.append(v_)
            si.append(i_)
        tm = sv[0][0].shape[1]
        cand = jnp.concatenate([sv[0][i] + sv[1][j] for i, j in pairs]
                               + [jnp.full((n_pad, tm), NEG_INF, F32)], axis=0)
        cexp = jnp.concatenate([si[0][i] * N_KEYS + si[1][j] for i, j in pairs]
                               + [jnp.zeros((n_pad, tm), jnp.int32)], axis=0)
        cs, ce = _top_rows(cand, PEER_TOPK, payload=cexp)
        ex = [jnp.exp(c_ - cs[0]) for c_ in cs]
        den = ex[0]
        for e_ in ex[1:]:
            den = den + e_
        experts += ce
        gates += [e_ / den for e_ in ex]
    e_ref[...] = jnp.concatenate(experts, axis=0).T
    gt_ref[...] = jnp.concatenate(gates, axis=0)


def _route(x2, attn, conv, w_out, g, wq, keys, tm):
    T, D = x2.shape
    aw = attn.shape[1]
    cw = conv.shape[1]
    qc = wq.shape[1]
    hk = N_PEER_HEADS * PEER_TOPK
    return pl.pallas_call(
        functools.partial(_route_kernel, aw=aw),
        grid=(T // tm,),
        in_specs=[
            pl.BlockSpec((tm, D), lambda i: (i, 0)),
            pl.BlockSpec((tm, aw), lambda i: (i, 0)),
            pl.BlockSpec((tm, cw), lambda i: (i, 0)),
            pl.BlockSpec((aw + cw, D), lambda i: (0, 0)),
            pl.BlockSpec((1, D), lambda i: (0, 0)),
            pl.BlockSpec((D, qc), lambda i: (0, 0)),
            pl.BlockSpec((2, N_KEYS, PEER_HALF), lambda i: (0, 0, 0)),
        ],
        out_specs=[
            pl.BlockSpec((tm, D), lambda i: (i, 0)),
            pl.BlockSpec((tm, D), lambda i: (i, 0)),
            pl.BlockSpec((tm, hk), lambda i: (i, 0)),
            pl.BlockSpec((hk, tm), lambda i: (0, i)),
        ],
        out_shape=[
            jax.ShapeDtypeStruct((T, D), F32),
            jax.ShapeDtypeStruct((T, D), F32),
            jax.ShapeDtypeStruct((T, hk), jnp.int32),
            jax.ShapeDtypeStruct((hk, T), F32),
        ],
        compiler_params=_params(("parallel",)),
        name="route",
    )(x2, attn, conv, w_out, g, wq, keys)


def _table_rows(tab):
    n, d = tab.shape
    return tab.astype(BF16).reshape(n, d // LANES, LANES)


def _peer_a_kernel(idx_ref, m_ref, gt_ref, tab_ref, h_ref, *scratch, tb, hk, sub):
    *p_parts, g_sc = scratch
    rows_per_part = hk // len(p_parts)
    lane = lax.broadcasted_iota(jnp.int32, (hk, tb), 1)

    def place(a_t, t):
        return jnp.where(lane == t, jnp.sum(g_sc[t], axis=1, keepdims=True), a_t)

    g_sc[0] = jnp.zeros(g_sc.shape[1:], F32)

    def token(t, a_t):
        a_t = place(a_t, jnp.maximum(t - 1, 0))
        x = m_ref[pl.ds(t, 1), :].reshape(sub, LANES)
        ids = idx_ref.at[t]

        def sublane_sums(q):
            p_sc = p_parts[q]
            g = p_sc[pl.ds(0, rows_per_part, stride=sub), :]
            for s in range(1, sub):
                g = g + p_sc[pl.ds(s, rows_per_part, stride=sub), :]
            g_sc[t, q * rows_per_part:(q + 1) * rows_per_part, :] = g

        for q, p_sc in enumerate(p_parts):
            for r in range(rows_per_part):
                p_sc[r * sub:(r + 1) * sub, :] = tab_ref[ids[q * rows_per_part + r]].astype(F32) * x
            if q > 0:
                sublane_sums(q - 1)
        sublane_sums(len(p_parts) - 1)
        return a_t

    a = place(lax.fori_loop(0, tb, token, jnp.zeros((hk, tb), F32)), tb - 1)
    h_ref[...] = (0.5 * a * (1.0 + lax.erf(a * (2.0 ** -0.5))) * gt_ref[...]).T


PEER_A_PARTS = 4


def _peer_a(idx, m, gates, tab, tb, start, tc):
    hk = idx.shape[1]
    sub = tab.shape[1]
    b0 = start // tb
    return pl.pallas_call(
        functools.partial(_peer_a_kernel, tb=tb, hk=hk, sub=sub),
        grid=(tc // tb,),
        in_specs=[
            pl.BlockSpec((tb, hk), lambda i: (b0 + i, 0), memory_space=pltpu.SMEM),
            pl.BlockSpec((tb, sub * LANES), lambda i: (b0 + i, 0)),
            pl.BlockSpec((hk, tb), lambda i: (0, b0 + i)),
            pl.BlockSpec(tab.shape, lambda i: (0, 0, 0), pipeline_mode=pl.Buffered(1)),
        ],
        out_specs=pl.BlockSpec((tb, hk), lambda i: (i, 0)),
        out_shape=jax.ShapeDtypeStruct((tc, hk), F32),
        scratch_shapes=[pltpu.VMEM((hk // PEER_A_PARTS * sub, LANES), F32)] * PEER_A_PARTS
        + [pltpu.VMEM((tb, hk, LANES), F32)],
        compiler_params=_params(("arbitrary",)),
        name="peer_a",
    )(idx, m, gates, tab)


SC_LANES = 16
SC_WORD_PAIR = 32
SC_ROW_BUFS = 4
SC_LOAD_GROUP = 4
SC_ROW_GROUP = 4


def _table_words(tab):
    n, d = tab.shape
    t = tab.astype(BF16).reshape(n, d // SC_WORD_PAIR, 2, SC_LANES).transpose(0, 1, 3, 2)
    return lax.bitcast_convert_type(t.reshape(n, d // 2, 2), jnp.int32)


def _peer_y_sc(idx, h, tab_words, start, tc):
    hk = idx.shape[1]
    dw = tab_words.shape[1]
    d = 2 * dw
    info = plsc.get_sparse_core_info()
    nc, ns = info.num_cores, info.num_subcores
    tpw = tc // (nc * ns)
    rq = hk // SC_ROW_BUFS
    n_vec = dw // SC_LANES
    assert tpw % 2 == 0 and n_vec % SC_LOAD_GROUP == 0 and rq % SC_ROW_GROUP == 0
    mesh = plsc.VectorSubcoreMesh(core_axis_name="c", subcore_axis_name="s")

    @functools.partial(
        pl.kernel, mesh=mesh,
        out_type=jax.ShapeDtypeStruct((tc, d), F32),
        scratch_types=[pltpu.VMEM((tpw, hk), jnp.int32), pltpu.VMEM((tpw, hk), F32)]
        + [pltpu.VMEM((rq, dw), jnp.int32)] * SC_ROW_BUFS
        + [pltpu.VMEM((d,), F32)] * 2
        + [pltpu.SemaphoreType.DMA] * (SC_ROW_BUFS + 2),
        compiler_params=pltpu.CompilerParams(needs_layout_passes=False),
        name="peer_y_sc",
    )
    def body(idx_hbm, h_hbm, tab_hbm, y_hbm, ids_v, w_v, *scratch):
        rows = scratch[:SC_ROW_BUFS]
        accs = scratch[SC_ROW_BUFS:SC_ROW_BUFS + 2]
        row_sems = scratch[SC_ROW_BUFS + 2:2 * SC_ROW_BUFS + 2]
        out_sems = scratch[2 * SC_ROW_BUFS + 2:]
        base = (lax.axis_index("s") * nc + lax.axis_index("c")) * tpw
        pltpu.sync_copy(idx_hbm.at[pl.ds(start + base, tpw)], ids_v)
        pltpu.sync_copy(h_hbm.at[pl.ds(base, tpw)], w_v)

        def gather(i, q):
            return pltpu.make_async_copy(tab_hbm.at[ids_v.at[i, pl.ds(q * rq, rq)]], rows[q], row_sems[q])

        def put(i, par):
            return pltpu.make_async_copy(accs[par], y_hbm.at[base + i], out_sems[par])

        def accumulate(i, q, acc):
            lanes0 = jnp.zeros((SC_LANES,), jnp.int32)

            n_groups = n_vec // SC_LOAD_GROUP

            @pl.loop(0, rq, step=SC_ROW_GROUP)
            def _(r0):
                ws = [plsc.load_gather(w_v, [lanes0 + i, lanes0 + (q * rq + r0 + j)]) for j in range(SC_ROW_GROUP)]

                def load(g):
                    return [[rows[q][r0 + j, pl.ds((g * SC_LOAD_GROUP + c) * SC_LANES, SC_LANES)]
                             for c in range(SC_LOAD_GROUP)] for j in range(SC_ROW_GROUP)]

                words = load(0)
                for g in range(n_groups):
                    nxt = load(g + 1) if g + 1 < n_groups else None
                    for c in range(SC_LOAD_GROUP):
                        lo_sum = hi_sum = None
                        for j in range(SC_ROW_GROUP):
                            lo, hi = plsc.unpack(plsc.bitcast(words[j][c], BF16), format=plsc.PackFormat.INTERLEAVED,
                                                 preferred_element_type=F32)
                            lo_sum = ws[j] * lo if lo_sum is None else lo_sum + ws[j] * lo
                            hi_sum = ws[j] * hi if hi_sum is None else hi_sum + ws[j] * hi
                        col = (g * SC_LOAD_GROUP + c) * SC_WORD_PAIR
                        plsc.addupdate(acc.at[pl.ds(col, SC_LANES)], lo_sum)
                        plsc.addupdate(acc.at[pl.ds(col + SC_LANES, SC_LANES)], hi_sum)
                    words = nxt

        for q in range(SC_ROW_BUFS):
            gather(0, q).start()

        @pl.loop(0, tpw, step=2)
        def _(i0):
            for par in range(2):
                i = i0 + par

                @pl.when(i0 > 0)
                def _():
                    put(i - 2, par).wait()

                zero = jnp.zeros((SC_LANES,), F32)
                for c in range(d // SC_LANES):
                    accs[par][pl.ds(c * SC_LANES, SC_LANES)] = zero
                for q in range(SC_ROW_BUFS):
                    gather(i, q).wait()
                    accumulate(i, q, accs[par])

                    @pl.when(i + 1 < tpw)
                    def _():
                        gather(i + 1, q).start()

                put(i, par).start()

        for par in range(2):
            put(tpw - 2 + par, par).wait()

    return body(idx, h, tab_words)


def _final_kernel(h_ref, y_ref, p_ref, wp_ref, gp_ref, wg_ref, gf_ref, *rest, apply_final):
    o_ref = rest[-1]
    h2 = h_ref[...] + y_ref[...]
    e = jnp.dot(p_ref[...].astype(BF16), wp_ref[...], preferred_element_type=F32)
    n = _rms(h2, gp_ref[...]).astype(BF16)
    gate = jax.nn.sigmoid(jnp.dot(n, wg_ref[...], preferred_element_type=F32))
    h3 = h2 + e * gate
    o_ref[...] = _rms(h3, gf_ref[...]) if apply_final else h3


def _ple(h1, y_chunk, p2, w_proj, g_ple, w_gate, g_final, apply_final, tm, start, out_prev):
    T, D = h1.shape
    size = y_chunk.shape[0]
    pd = p2.shape[1]
    b0 = start // tm
    in_specs = [
        pl.BlockSpec((tm, D), lambda i: (b0 + i, 0)),
        pl.BlockSpec((tm, D), lambda i: (i, 0)),
        pl.BlockSpec((tm, pd), lambda i: (b0 + i, 0)),
        pl.BlockSpec((pd, D), lambda i: (0, 0)),
        pl.BlockSpec((1, D), lambda i: (0, 0)),
        pl.BlockSpec((D, D), lambda i: (0, 0)),
        pl.BlockSpec((1, D), lambda i: (0, 0)),
    ]
    args = [h1, y_chunk, p2, w_proj, g_ple, w_gate, g_final]
    aliases = {}
    if out_prev is not None:
        in_specs.append(pl.BlockSpec(memory_space=pl.ANY))
        args.append(out_prev)
        aliases = {len(args) - 1: 0}
    return pl.pallas_call(
        functools.partial(_final_kernel, apply_final=apply_final),
        grid=(size // tm,),
        in_specs=in_specs,
        out_specs=pl.BlockSpec((tm, D), lambda i: (b0 + i, 0)),
        out_shape=jax.ShapeDtypeStruct((T, D), F32),
        input_output_aliases=aliases,
        compiler_params=_params(("parallel",)),
        name="ple_final",
    )(*args)


def _qk_permutation():
    half = ATTN_HEAD_DIM // 2
    perm = np.zeros(N_ATTN_HEADS * LANES, np.int32)
    for h in range(N_ATTN_HEADS):
        for c in range(2):
            for d in range(ATTN_HEAD_DIM):
                perm[h * LANES + (d // half) * 2 * half + c * half + d % half] = h * LANES + c * ATTN_HEAD_DIM + d
    return perm


def _rope_tables(seq):
    half = ATTN_HEAD_DIM // 2
    inv_freq = 1.0 / (ROPE_THETA ** (jnp.arange(half, dtype=F32) * 2.0 / ATTN_HEAD_DIM))
    ang = jnp.arange(seq).astype(F32)[:, None] * inv_freq[None, :]
    cos, sin = jnp.cos(ang), jnp.sin(ang)
    return jnp.tile(cos, (1, 4)), jnp.concatenate([-sin, -sin, sin, sin], axis=1)


def _token_chunks(total, big):
    sizes = [big] * (total // big)
    if big % 1024 == 0:
        sizes[-1:] = [big // 2, big // 4, big // 4]
    return sizes


def kernel(x, p, attn_norm_g, w_in, lambda_q1, lambda_k1, lambda_q2, lambda_k2, subln_g, conv_w, conv_b,
           conv_ln_g, conv_ln_b, w_out, ffn_norm_g, peer_wq, peer_keys, peer_u, peer_v, ple_norm_g,
           ple_w_gate, ple_w_proj, final_norm_g):
    B, S, D = x.shape
    T = B * S
    depth = w_in.shape[0]
    assert depth >= 1
    tm = min(512, S)
    tq = min(512, S)
    tr = min(256, S)
    tb = min(128, T)
    tc = min(4096, T)
    qk_half = N_ATTN_HEADS * 2 * ATTN_HEAD_DIM

    perm = _qk_permutation()
    col_order = np.concatenate([perm, qk_half + perm, np.arange(2 * qk_half, w_in.shape[2])])
    cos_t, sin_t = _rope_tables(S)
    row = lambda a: a.reshape(1, -1).astype(F32)

    h = x.reshape(T, D)
    for l in range(depth):
        lambda_init = 0.8 - 0.6 * math.exp(-0.3 * l)
        w_perm = w_in[l][:, col_order].astype(BF16)
        qk, v, cvg = _in_proj(h, row(attn_norm_g[l]), w_perm, cos_t, sin_t, S, tm)
        lamv = jnp.stack([lambda_q1[l], lambda_k1[l], lambda_q2[l], lambda_k2[l]]).astype(F32)
        attn = _diff_attn(qk, v, lamv, row(subln_g[l]), B, S, tq, lambda_init)
        conv = _conv_module(cvg, conv_w[l], row(conv_b[l]), row(conv_ln_g[l]), row(conv_ln_b[l]), S, tm)
        h1, m, experts, gates = _route(h, attn, conv, w_out[l].astype(BF16), row(ffn_norm_g[l]),
                                       peer_wq[l].astype(BF16), peer_keys[l].astype(BF16), tr)
        u_rows, v_words = _table_rows(peer_u[l]), _table_words(peer_v[l])
        ple_args = (p[l].reshape(T, -1), ple_w_proj[l].astype(BF16), row(ple_norm_g[l]), ple_w_gate[l].astype(BF16),
                    row(final_norm_g), l == depth - 1, tm)
        ys, start = [], 0
        for size in _token_chunks(T, tc):
            hw = _peer_a(experts, m, gates, u_rows, tb, start, size)
            ys.append((start, _peer_y_sc(experts, hw, v_words, start, size)))
            start += size
        h = None
        for start, y_chunk in ys:
            h = _ple(h1, y_chunk, *ple_args, start, h)
    return h.reshape(B, S, D)
```

```python
import functools
import math

import numpy as np
import jax
import jax.numpy as jnp
from jax import lax
from jax.experimental import pallas as pl
from jax.experimental.pallas import tpu as pltpu
from jax.experimental.pallas import tpu_sc as plsc

EPS = 1e-6
ROPE_THETA = 10000.0
ATTN_HEAD_DIM = 64
N_ATTN_HEADS = 4
CONV_KERNEL = 31
N_PEER_HEADS = 8
N_KEYS = 128
PEER_TOPK = 16
PEER_HALF = 128

LANES = 128
CONV_HALO = 32
VMEM_LIMIT = 56 * 1024 * 1024

F32 = jnp.float32
BF16 = jnp.bfloat16
NEG_INF = float("-inf")


def _rms(x, g):
    ms = jnp.mean(x * x, axis=-1, keepdims=True)
    return x * lax.rsqrt(ms + EPS) * g


def _params(sem):
    return pltpu.CompilerParams(dimension_semantics=sem, vmem_limit_bytes=VMEM_LIMIT)


def _inproj_kernel(x_ref, g_ref, w_ref, cos_ref, sin_ref, qk_ref, v_ref, c_ref, *, qk_cols, v_cols):
    a = _rms(x_ref[...], g_ref[...]).astype(BF16)
    cos = cos_ref[...]
    sin = sin_ref[...]
    for col in range(0, qk_cols, 2 * LANES):
        z = jnp.dot(a, w_ref[:, col:col + 2 * LANES], preferred_element_type=F32)
        for half in range(2):
            zz = z[:, half * LANES:(half + 1) * LANES]
            r = zz * cos + pltpu.roll(zz, LANES // 2, 1) * sin
            c0 = col + half * LANES
            if c0 < qk_cols // 2:
                r = r * (ATTN_HEAD_DIM ** -0.5)
            qk_ref[:, c0:c0 + LANES] = r.astype(BF16)
    v_ref[...] = jnp.dot(a, w_ref[:, qk_cols:qk_cols + v_cols], preferred_element_type=F32).astype(BF16)
    c_ref[...] = jnp.dot(a, w_ref[:, qk_cols + v_cols:], preferred_element_type=F32)


def _in_proj(x2, g, w_perm, cos_t, sin_t, seq, tm):
    T, D = x2.shape
    n_cols = w_perm.shape[1]
    qk_cols = 2 * N_ATTN_HEADS * 2 * ATTN_HEAD_DIM
    v_cols = N_ATTN_HEADS * 2 * ATTN_HEAD_DIM
    c_cols = n_cols - qk_cols - v_cols
    spt = seq // tm
    return pl.pallas_call(
        functools.partial(_inproj_kernel, qk_cols=qk_cols, v_cols=v_cols),
        grid=(T // tm,),
        in_specs=[
            pl.BlockSpec((tm, D), lambda i: (i, 0)),
            pl.BlockSpec((1, D), lambda i: (0, 0)),
            pl.BlockSpec((D, n_cols), lambda i: (0, 0)),
            pl.BlockSpec((tm, LANES), lambda i: (i % spt, 0)),
            pl.BlockSpec((tm, LANES), lambda i: (i % spt, 0)),
        ],
        out_specs=[
            pl.BlockSpec((tm, qk_cols), lambda i: (i, 0)),
            pl.BlockSpec((tm, v_cols), lambda i: (i, 0)),
            pl.BlockSpec((tm, c_cols), lambda i: (i, 0)),
        ],
        out_shape=[
            jax.ShapeDtypeStruct((T, qk_cols), BF16),
            jax.ShapeDtypeStruct((T, v_cols), BF16),
            jax.ShapeDtypeStruct((T, c_cols), F32),
        ],
        compiler_params=_params(("parallel",)),
        name="in_proj",
    )(x2, g, w_perm, cos_t, sin_t)


def _attn_kernel(q_ref, k_ref, v_ref, lam_ref, g_ref, o_ref, m_sc, l_sc, acc_sc, *, tq, lambda_init):
    i = pl.program_id(2)
    q = q_ref[...]
    lane = lax.broadcasted_iota(jnp.int32, (1, LANES), 1)
    map0 = (lane // (ATTN_HEAD_DIM // 2)) % 2 == 0
    zero = jnp.zeros_like(q)
    qs = jnp.concatenate([jnp.where(map0, q, zero), jnp.where(map0, zero, q)], axis=0)

    m_sc[...] = jnp.full(m_sc.shape, NEG_INF, F32)
    l_sc[...] = jnp.zeros(l_sc.shape, F32)
    acc_sc[...] = jnp.zeros(acc_sc.shape, F32)

    def step(start, masked):
        k = k_ref[pl.ds(start, tq), :]
        v = v_ref[pl.ds(start, tq), :]
        s = lax.dot_general(qs, k, (((1,), (1,)), ((), ())), preferred_element_type=F32)
        if masked:
            row = lax.broadcasted_iota(jnp.int32, (2 * tq, tq), 0)
            col = lax.broadcasted_iota(jnp.int32, (2 * tq, tq), 1)
            s = jnp.where(col <= row % tq, s, NEG_INF)
        m_prev = m_sc[...]
        m_new = jnp.maximum(m_prev, jnp.max(s, axis=1, keepdims=True))
        alpha = jnp.exp(m_prev - m_new)
        p = jnp.exp(s - jnp.tile(m_new, (1, tq // LANES)))
        l_sc[...] = alpha * l_sc[...] + jnp.sum(p, axis=1, keepdims=True)
        acc_sc[...] = alpha * acc_sc[...] + jnp.dot(p.astype(BF16), v, preferred_element_type=F32)
        m_sc[...] = m_new

    def body(j, carry):
        step(pl.multiple_of(j * tq, tq), False)
        return carry

    lax.fori_loop(0, i, body, 0)
    step(pl.multiple_of(i * tq, tq), True)

    lv = lam_ref[...]
    lam = (jnp.exp(jnp.sum(lv[0:1] * lv[1:2], axis=1, keepdims=True))
           - jnp.exp(jnp.sum(lv[2:3] * lv[3:4], axis=1, keepdims=True)) + lambda_init)
    o = acc_sc[0:tq, :] / l_sc[0:tq, :] - lam * (acc_sc[tq:, :] / l_sc[tq:, :])
    o_ref[...] = (_rms(o, g_ref[...]) * (1.0 - lambda_init)).astype(o_ref.dtype)


def _diff_attn(qk, v, lamv, subln_g, batch, seq, tq, lambda_init):
    T = qk.shape[0]
    H = N_ATTN_HEADS
    nq = seq // tq
    return pl.pallas_call(
        functools.partial(_attn_kernel, tq=tq, lambda_init=lambda_init),
        grid=(batch, H, nq),
        in_specs=[
            pl.BlockSpec((tq, LANES), lambda b, h, i: (b * nq + i, h)),
            pl.BlockSpec((seq, LANES), lambda b, h, i: (b, H + h)),
            pl.BlockSpec((seq, LANES), lambda b, h, i: (b, h)),
            pl.BlockSpec((4, ATTN_HEAD_DIM), lambda b, h, i: (0, 0)),
            pl.BlockSpec((1, LANES), lambda b, h, i: (0, 0)),
        ],
        out_specs=pl.BlockSpec((tq, LANES), lambda b, h, i: (b * nq + i, h)),
        out_shape=jax.ShapeDtypeStruct((T, H * LANES), BF16),
        scratch_shapes=[
            pltpu.VMEM((2 * tq, LANES), F32),
            pltpu.VMEM((2 * tq, LANES), F32),
            pltpu.VMEM((2 * tq, LANES), F32),
        ],
        compiler_params=_params(("parallel", "parallel", "arbitrary")),
        name="diff_attn",
    )(qk, qk, v, lamv, subln_g)


def _conv_kernel(cur_ref, prev_ref, w_ref, b_ref, g_ref, beta_ref, o_ref, u_sc, *, ts, tiles_per_seq, width, rows):
    i = pl.program_id(0)
    first = (i % tiles_per_seq) == 0
    pc = prev_ref[...]
    up = pc[:, :width] * jax.nn.sigmoid(pc[:, width:])
    u_sc[0:CONV_HALO, :] = jnp.where(first, jnp.zeros_like(up), up)
    c = cur_ref[...]
    u_sc[CONV_HALO:, :] = c[:, :width] * jax.nn.sigmoid(c[:, width:])
    bias = b_ref[...]
    gam = g_ref[...]
    beta = beta_ref[...]
    shift = CONV_HALO - (CONV_KERNEL - 1)

    def chunk(r, carry):
        r0 = pl.multiple_of(r * rows, rows)
        acc = jnp.broadcast_to(bias, (rows, width))
        win = u_sc[pl.ds(r0, rows + CONV_HALO), :]
        for b in range(8):
            taps = [j for j in range(CONV_KERNEL) if (shift + j) % 8 == b]
            if not taps:
                continue
            hi = max(shift + j for j in taps) - b
            wb = win[b:b + hi + rows, :]
            for j in taps:
                a8 = shift + j - b
                acc = acc + w_ref[j:j + 1, :] * wb[a8:a8 + rows, :]
        mu = jnp.mean(acc, axis=-1, keepdims=True)
        d = acc - mu
        var = jnp.mean(d * d, axis=-1, keepdims=True)
        y = d * lax.rsqrt(var + EPS) * gam + beta
        o_ref[pl.ds(r0, rows), :] = (y * jax.nn.sigmoid(y)).astype(o_ref.dtype)
        return carry

    lax.fori_loop(0, ts // rows, chunk, 0)


def _conv_module(cvg, w, b, ln_g, ln_b, seq, ts):
    T, two_w = cvg.shape
    width = two_w // 2
    rows = 32
    hb = ts // CONV_HALO
    return pl.pallas_call(
        functools.partial(_conv_kernel, ts=ts, tiles_per_seq=seq // ts, width=width, rows=rows),
        grid=(T // ts,),
        in_specs=[
            pl.BlockSpec((ts, two_w), lambda i: (i, 0)),
            pl.BlockSpec((CONV_HALO, two_w), lambda i: (jnp.maximum(i * hb - 1, 0), 0)),
            pl.BlockSpec((CONV_KERNEL, width), lambda i: (0, 0)),
            pl.BlockSpec((1, width), lambda i: (0, 0)),
            pl.BlockSpec((1, width), lambda i: (0, 0)),
            pl.BlockSpec((1, width), lambda i: (0, 0)),
        ],
        out_specs=pl.BlockSpec((ts, width), lambda i: (i, 0)),
        out_shape=jax.ShapeDtypeStruct((T, width), BF16),
        scratch_shapes=[pltpu.VMEM((ts + CONV_HALO, width), F32)],
        compiler_params=_params(("parallel",)),
        name="conv_module",
    )(cvg, cvg, w, b, ln_g, ln_b)


def _top_rows(s, n_pick, payload=None):
    n = s.shape[0]
    iota = lax.broadcasted_iota(jnp.int32, s.shape, 0).astype(F32)
    vals, picks = [], []
    for _ in range(n_pick):
        mx = jnp.max(s, axis=0, keepdims=True)
        pos = jnp.min(jnp.where(s == mx, iota, float(n)), axis=0, keepdims=True)
        sel = iota == pos
        vals.append(mx)
        if payload is None:
            picks.append(pos.astype(jnp.int32))
        else:
            picks.append(jnp.sum(jnp.where(sel, payload, 0), axis=0, keepdims=True))
        s = jnp.where(sel, NEG_INF, s)
    return vals, picks


def _route_kernel(x_ref, at_ref, cv_ref, wo_ref, g_ref, wq_ref, keys_ref, h_ref, m_ref, e_ref, gt_ref, *, aw):
    h1 = (x_ref[...]
          + jnp.dot(at_ref[...], wo_ref[0:aw, :], preferred_element_type=F32)
          + jnp.dot(cv_ref[...], wo_ref[aw:, :], preferred_element_type=F32))
    h_ref[...] = h1
    m = _rms(h1, g_ref[...])
    m_ref[...] = m
    mb = m.astype(BF16)
    pairs = [(i, j) for i in range(PEER_TOPK) for j in range(PEER_TOPK) if (i + 1) * (j + 1) <= PEER_TOPK]
    n_pad = -len(pairs) % 8
    experts, gates = [], []
    for hd in range(N_PEER_HEADS):
        sv, si = [], []
        for c in range(2):
            c0 = (hd * 2 + c) * PEER_HALF
            q = jnp.dot(mb, wq_ref[:, c0:c0 + PEER_HALF], preferred_element_type=F32).astype(BF16)
            st = lax.dot_general(keys_ref[c], q, (((1,), (1,)), ((), ())), preferred_element_type=F32)
            v_, i_ = _top_rows(st, PEER_TOPK)
            sv.append(v_)
            si.append(i_)
        tm = sv[0][0].shape[1]
        cand = jnp.concatenate([sv[0][i] + sv[1][j] for i, j in pairs]
                               + [jnp.full((n_pad, tm), NEG_INF, F32)], axis=0)
        cexp = jnp.concatenate([si[0][i] * N_KEYS + si[1][j] for i, j in pairs]
                               + [jnp.zeros((n_pad, tm), jnp.int32)], axis=0)
        cs, ce = _top_rows(cand, PEER_TOPK, payload=cexp)
        ex = [jnp.exp(c_ - cs[0]) for c_ in cs]
        den = ex[0]
        for e_ in ex[1:]:
            den = den + e_
        experts += ce
        gates += [e_ / den for e_ in ex]
    e_ref[...] = jnp.concatenate(experts, axis=0).T
    gt_ref[...] = jnp.concatenate(gates, axis=0)


def _route(x2, attn, conv, w_out, g, wq, keys, tm):
    T, D = x2.shape
    aw = attn.shape[1]
    cw = conv.shape[1]
    qc = wq.shape[1]
    hk = N_PEER_HEADS * PEER_TOPK
    return pl.pallas_call(
        functools.partial(_route_kernel, aw=aw),
        grid=(T // tm,),
        in_specs=[
            pl.BlockSpec((tm, D), lambda i: (i, 0)),
            pl.BlockSpec((tm, aw), lambda i: (i, 0)),
            pl.BlockSpec((tm, cw), lambda i: (i, 0)),
            pl.BlockSpec((aw + cw, D), lambda i: (0, 0)),
            pl.BlockSpec((1, D), lambda i: (0, 0)),
            pl.BlockSpec((D, qc), lambda i: (0, 0)),
            pl.BlockSpec((2, N_KEYS, PEER_HALF), lambda i: (0, 0, 0)),
        ],
        out_specs=[
            pl.BlockSpec((tm, D), lambda i: (i, 0)),
            pl.BlockSpec((tm, D), lambda i: (i, 0)),
            pl.BlockSpec((tm, hk), lambda i: (i, 0)),
            pl.BlockSpec((hk, tm), lambda i: (0, i)),
        ],
        out_shape=[
            jax.ShapeDtypeStruct((T, D), F32),
            jax.ShapeDtypeStruct((T, D), F32),
            jax.ShapeDtypeStruct((T, hk), jnp.int32),
            jax.ShapeDtypeStruct((hk, T), F32),
        ],
        compiler_params=_params(("parallel",)),
        name="route",
    )(x2, attn, conv, w_out, g, wq, keys)


def _table_rows(tab):
    n, d = tab.shape
    return tab.astype(BF16).reshape(n, d // LANES, LANES)


def _peer_a_kernel(idx_ref, m_ref, gt_ref, tab_ref, h_ref, *scratch, tb, hk, sub):
    *p_parts, g_sc = scratch
    rows_per_part = hk // len(p_parts)
    lane = lax.broadcasted_iota(jnp.int32, (hk, tb), 1)

    def place(a_t, t):
        return jnp.where(lane == t, jnp.sum(g_sc[t], axis=1, keepdims=True), a_t)

    g_sc[0] = jnp.zeros(g_sc.shape[1:], F32)

    def token(t, a_t):
        a_t = place(a_t, jnp.maximum(t - 1, 0))
        x = m_ref[pl.ds(t, 1), :].reshape(sub, LANES)
        ids = idx_ref.at[t]

        def sublane_sums(q):
            p_sc = p_parts[q]
            g = p_sc[pl.ds(0, rows_per_part, stride=sub), :]
            for s in range(1, sub):
                g = g + p_sc[pl.ds(s, rows_per_part, stride=sub), :]
            g_sc[t, q * rows_per_part:(q + 1) * rows_per_part, :] = g

        for q, p_sc in enumerate(p_parts):
            for r in range(rows_per_part):
                p_sc[r * sub:(r + 1) * sub, :] = tab_ref[ids[q * rows_per_part + r]].astype(F32) * x
            if q > 0:
                sublane_sums(q - 1)
        sublane_sums(len(p_parts) - 1)
        return a_t

    a = place(lax.fori_loop(0, tb, token, jnp.zeros((hk, tb), F32)), tb - 1)
    h_ref[...] = (0.5 * a * (1.0 + lax.erf(a * (2.0 ** -0.5))) * gt_ref[...]).T


PEER_A_PARTS = 4


def _peer_a(idx, m, gates, tab, tb, start, tc):
    hk = idx.shape[1]
    sub = tab.shape[1]
    b0 = start // tb
    return pl.pallas_call(
        functools.partial(_peer_a_kernel, tb=tb, hk=hk, sub=sub),
        grid=(tc // tb,),
        in_specs=[
            pl.BlockSpec((tb, hk), lambda i: (b0 + i, 0), memory_space=pltpu.SMEM),
            pl.BlockSpec((tb, sub * LANES), lambda i: (b0 + i, 0)),
            pl.BlockSpec((hk, tb), lambda i: (0, b0 + i)),
            pl.BlockSpec(tab.shape, lambda i: (0, 0, 0), pipeline_mode=pl.Buffered(1)),
        ],
        out_specs=pl.BlockSpec((tb, hk), lambda i: (i, 0)),
        out_shape=jax.ShapeDtypeStruct((tc, hk), F32),
        scratch_shapes=[pltpu.VMEM((hk // PEER_A_PARTS * sub, LANES), F32)] * PEER_A_PARTS
        + [pltpu.VMEM((tb, hk, LANES), F32)],
        compiler_params=_params(("arbitrary",)),
        name="peer_a",
    )(idx, m, gates, tab)


SC_LANES = 16
SC_WORD_PAIR = 32
SC_ROW_BUFS = 4
SC_LOAD_GROUP = 4
SC_ROW_GROUP = 4


def _table_words(tab):
    n, d = tab.shape
    t = tab.astype(BF16).reshape(n, d // SC_WORD_PAIR, 2, SC_LANES).transpose(0, 1, 3, 2)
    return lax.bitcast_convert_type(t.reshape(n, d // 2, 2), jnp.int32)


def _peer_y_sc(idx, h, tab_words, start, tc):
    hk = idx.shape[1]
    dw = tab_words.shape[1]
    d = 2 * dw
    info = plsc.get_sparse_core_info()
    nc, ns = info.num_cores, info.num_subcores
    tpw = tc // (nc * ns)
    rq = hk // SC_ROW_BUFS
    n_vec = dw // SC_LANES
    assert tc % (2 * nc * ns) == 0 and n_vec % SC_LOAD_GROUP == 0 and rq % SC_ROW_GROUP == 0
    mesh = plsc.VectorSubcoreMesh(core_axis_name="c", subcore_axis_name="s")

    @functools.partial(
        pl.kernel, mesh=mesh,
        out_type=jax.ShapeDtypeStruct((tc, d), F32),
        scratch_types=[pltpu.VMEM((tpw, hk), jnp.int32), pltpu.VMEM((tpw, hk), F32)]
        + [pltpu.VMEM((rq, dw), jnp.int32)] * SC_ROW_BUFS
        + [pltpu.VMEM((d,), F32)] * 2
        + [pltpu.SemaphoreType.DMA] * (SC_ROW_BUFS + 2),
        compiler_params=pltpu.CompilerParams(needs_layout_passes=False),
        name="peer_y_sc",
    )
    def body(idx_hbm, h_hbm, tab_hbm, y_hbm, ids_v, w_v, *scratch):
        rows = scratch[:SC_ROW_BUFS]
        accs = scratch[SC_ROW_BUFS:SC_ROW_BUFS + 2]
        row_sems = scratch[SC_ROW_BUFS + 2:2 * SC_ROW_BUFS + 2]
        out_sems = scratch[2 * SC_ROW_BUFS + 2:]
        base = (lax.axis_index("s") * nc + lax.axis_index("c")) * tpw
        pltpu.sync_copy(idx_hbm.at[pl.ds(start + base, tpw)], ids_v)
        pltpu.sync_copy(h_hbm.at[pl.ds(base, tpw)], w_v)

        def gather(i, q):
            return pltpu.make_async_copy(tab_hbm.at[ids_v.at[i, pl.ds(q * rq, rq)]], rows[q], row_sems[q])

        def put(i, par):
            return pltpu.make_async_copy(accs[par], y_hbm.at[base + i], out_sems[par])

        def accumulate(i, q, acc):
            lanes0 = jnp.zeros((SC_LANES,), jnp.int32)

            n_groups = n_vec // SC_LOAD_GROUP

            @pl.loop(0, rq, step=SC_ROW_GROUP)
            def _(r0):
                ws = [plsc.load_gather(w_v, [lanes0 + i, lanes0 + (q * rq + r0 + j)]) for j in range(SC_ROW_GROUP)]

                def load(g):
                    return [[rows[q][r0 + j, pl.ds((g * SC_LOAD_GROUP + c) * SC_LANES, SC_LANES)]
                             for c in range(SC_LOAD_GROUP)] for j in range(SC_ROW_GROUP)]

                words = load(0)
                for g in range(n_groups):
                    nxt = load(g + 1) if g + 1 < n_groups else None
                    for c in range(SC_LOAD_GROUP):
                        lo_sum = hi_sum = None
                        for j in range(SC_ROW_GROUP):
                            lo, hi = plsc.unpack(plsc.bitcast(words[j][c], BF16), format=plsc.PackFormat.INTERLEAVED,
                                                 preferred_element_type=F32)
                            lo_sum = ws[j] * lo if lo_sum is None else lo_sum + ws[j] * lo
                            hi_sum = ws[j] * hi if hi_sum is None else hi_sum + ws[j] * hi
                        col = (g * SC_LOAD_GROUP + c) * SC_WORD_PAIR
                        plsc.addupdate(acc.at[pl.ds(col, SC_LANES)], lo_sum)
                        plsc.addupdate(acc.at[pl.ds(col + SC_LANES, SC_LANES)], hi_sum)
                    words = nxt

        for q in range(SC_ROW_BUFS):
            gather(0, q).start()

        @pl.loop(0, tpw, step=2)
        def _(i0):
            for par in range(2):
                i = i0 + par

                @pl.when(i0 > 0)
                def _():
                    put(i - 2, par).wait()

                zero = jnp.zeros((SC_LANES,), F32)
                for c in range(d // SC_LANES):
                    accs[par][pl.ds(c * SC_LANES, SC_LANES)] = zero
                for q in range(SC_ROW_BUFS):
                    gather(i, q).wait()
                    accumulate(i, q, accs[par])

                    @pl.when(i + 1 < tpw)
                    def _():
                        gather(i + 1, q).start()

                put(i, par).start()

        for par in range(2):
            put(tpw - 2 + par, par).wait()

    return body(idx, h, tab_words)


def _final_kernel(h_ref, y_ref, p_ref, wp_ref, gp_ref, wg_ref, gf_ref, *rest, apply_final):
    o_ref = rest[-1]
    h2 = h_ref[...] + y_ref[...]
    e = jnp.dot(p_ref[...].astype(BF16), wp_ref[...], preferred_element_type=F32)
    n = _rms(h2, gp_ref[...]).astype(BF16)
    gate = jax.nn.sigmoid(jnp.dot(n, wg_ref[...], preferred_element_type=F32))
    h3 = h2 + e * gate
    o_ref[...] = _rms(h3, gf_ref[...]) if apply_final else h3


def _ple(h1, y_chunk, p2, w_proj, g_ple, w_gate, g_final, apply_final, tm, start, out_prev):
    T, D = h1.shape
    size = y_chunk.shape[0]
    pd = p2.shape[1]
    b0 = start // tm
    in_specs = [
        pl.BlockSpec((tm, D), lambda i: (b0 + i, 0)),
        pl.BlockSpec((tm, D), lambda i: (i, 0)),
        pl.BlockSpec((tm, pd), lambda i: (b0 + i, 0)),
        pl.BlockSpec((pd, D), lambda i: (0, 0)),
        pl.BlockSpec((1, D), lambda i: (0, 0)),
        pl.BlockSpec((D, D), lambda i: (0, 0)),
        pl.BlockSpec((1, D), lambda i: (0, 0)),
    ]
    args = [h1, y_chunk, p2, w_proj, g_ple, w_gate, g_final]
    aliases = {}
    if out_prev is not None:
        in_specs.append(pl.BlockSpec(memory_space=pl.ANY))
        args.append(out_prev)
        aliases = {len(args) - 1: 0}
    return pl.pallas_call(
        functools.partial(_final_kernel, apply_final=apply_final),
        grid=(size // tm,),
        in_specs=in_specs,
        out_specs=pl.BlockSpec((tm, D), lambda i: (b0 + i, 0)),
        out_shape=jax.ShapeDtypeStruct((T, D), F32),
        input_output_aliases=aliases,
        compiler_params=_params(("parallel",)),
        name="ple_final",
    )(*args)


def _qk_permutation():
    half = ATTN_HEAD_DIM // 2
    perm = np.zeros(N_ATTN_HEADS * LANES, np.int32)
    for h in range(N_ATTN_HEADS):
        for c in range(2):
            for d in range(ATTN_HEAD_DIM):
                perm[h * LANES + (d // half) * 2 * half + c * half + d % half] = h * LANES + c * ATTN_HEAD_DIM + d
    return perm


def _rope_tables(seq):
    half = ATTN_HEAD_DIM // 2
    inv_freq = 1.0 / (ROPE_THETA ** (jnp.arange(half, dtype=F32) * 2.0 / ATTN_HEAD_DIM))
    ang = jnp.arange(seq).astype(F32)[:, None] * inv_freq[None, :]
    cos, sin = jnp.cos(ang), jnp.sin(ang)
    return jnp.tile(cos, (1, 4)), jnp.concatenate([-sin, -sin, sin, sin], axis=1)


def _token_chunks(total, big):
    sizes = [big] * (total // big)
    if big % 1024 == 0:
        sizes[-1:] = [big // 2, big // 4, big // 4]
    return sizes


def kernel(x, p, attn_norm_g, w_in, lambda_q1, lambda_k1, lambda_q2, lambda_k2, subln_g, conv_w, conv_b,
           conv_ln_g, conv_ln_b, w_out, ffn_norm_g, peer_wq, peer_keys, peer_u, peer_v, ple_norm_g,
           ple_w_gate, ple_w_proj, final_norm_g):
    B, S, D = x.shape
    T = B * S
    depth = w_in.shape[0]
    assert depth >= 1
    tm = min(512, S)
    tq = min(512, S)
    tr = min(256, S)
    tb = min(128, T)
    tc = min(4096, T)
    qk_half = N_ATTN_HEADS * 2 * ATTN_HEAD_DIM

    perm = _qk_permutation()
    col_order = np.concatenate([perm, qk_half + perm, np.arange(2 * qk_half, w_in.shape[2])])
    cos_t, sin_t = _rope_tables(S)
    row = lambda a: a.reshape(1, -1).astype(F32)

    h = x.reshape(T, D)
    for l in range(depth):
        lambda_init = 0.8 - 0.6 * math.exp(-0.3 * l)
        w_perm = w_in[l][:, col_order].astype(BF16)
        qk, v, cvg = _in_proj(h, row(attn_norm_g[l]), w_perm, cos_t, sin_t, S, tm)
        lamv = jnp.stack([lambda_q1[l], lambda_k1[l], lambda_q2[l], lambda_k2[l]]).astype(F32)
        attn = _diff_attn(qk, v, lamv, row(subln_g[l]), B, S, tq, lambda_init)
        conv = _conv_module(cvg, conv_w[l], row(conv_b[l]), row(conv_ln_g[l]), row(conv_ln_b[l]), S, tm)
        h1, m, experts, gates = _route(h, attn, conv, w_out[l].astype(BF16), row(ffn_norm_g[l]),
                                       peer_wq[l].astype(BF16), peer_keys[l].astype(BF16), tr)
        u_rows, v_words = _table_rows(peer_u[l]), _table_words(peer_v[l])
        ple_args = (p[l].reshape(T, -1), ple_w_proj[l].astype(BF16), row(ple_norm_g[l]), ple_w_gate[l].astype(BF16),
                    row(final_norm_g), l == depth - 1, tm)
        sizes = _token_chunks(T, tc)
        assert sum(sizes) == T and all(size % tm == 0 and size % tb == 0 for size in sizes), (T, sizes)
        ys, start = [], 0
        for size in sizes:
            hw = _peer_a(experts, m, gates, u_rows, tb, start, size)
            ys.append((start, _peer_y_sc(experts, hw, v_words, start, size)))
            start += size
        h = None
        for start, y_chunk in ys:
            h = _ple(h1, y_chunk, *ple_args, start, h)
    return h.reshape(B, S, D)
```

```python
import functools
import math

import numpy as np
import jax
import jax.numpy as jnp
from jax import lax
from jax.experimental import pallas as pl
from jax.experimental.pallas import tpu as pltpu
from jax.experimental.pallas import tpu_sc as plsc

EPS = 1e-6
ROPE_THETA = 10000.0
ATTN_HEAD_DIM = 64
N_ATTN_HEADS = 4
CONV_KERNEL = 31
N_PEER_HEADS = 8
N_KEYS = 128
PEER_TOPK = 16
PEER_HALF = 128

LANES = 128
CONV_HALO = 32
VMEM_LIMIT = 56 * 1024 * 1024

F32 = jnp.float32
BF16 = jnp.bfloat16
NEG_INF = float("-inf")


def _rms(x, g):
    ms = jnp.mean(x * x, axis=-1, keepdims=True)
    return x * lax.rsqrt(ms + EPS) * g


def _params(sem):
    return pltpu.CompilerParams(dimension_semantics=sem, vmem_limit_bytes=VMEM_LIMIT)


def _inproj_kernel(x_ref, g_ref, w_ref, cos_ref, sin_ref, qk_ref, v_ref, c_ref, *, qk_cols, v_cols):
    a = _rms(x_ref[...], g_ref[...]).astype(BF16)
    cos = cos_ref[...]
    sin = sin_ref[...]
    for col in range(0, qk_cols, 2 * LANES):
        z = jnp.dot(a, w_ref[:, col:col + 2 * LANES], preferred_element_type=F32)
        for half in range(2):
            zz = z[:, half * LANES:(half + 1) * LANES]
            r = zz * cos + pltpu.roll(zz, LANES // 2, 1) * sin
            c0 = col + half * LANES
            if c0 < qk_cols // 2:
                r = r * (ATTN_HEAD_DIM ** -0.5)
            qk_ref[:, c0:c0 + LANES] = r.astype(BF16)
    v_ref[...] = jnp.dot(a, w_ref[:, qk_cols:qk_cols + v_cols], preferred_element_type=F32).astype(BF16)
    c_ref[...] = jnp.dot(a, w_ref[:, qk_cols + v_cols:], preferred_element_type=F32)


def _in_proj(x2, g, w_perm, cos_t, sin_t, seq, tm):
    T, D = x2.shape
    n_cols = w_perm.shape[1]
    qk_cols = 2 * N_ATTN_HEADS * 2 * ATTN_HEAD_DIM
    v_cols = N_ATTN_HEADS * 2 * ATTN_HEAD_DIM
    c_cols = n_cols - qk_cols - v_cols
    spt = seq // tm
    return pl.pallas_call(
        functools.partial(_inproj_kernel, qk_cols=qk_cols, v_cols=v_cols),
        grid=(T // tm,),
        in_specs=[
            pl.BlockSpec((tm, D), lambda i: (i, 0)),
            pl.BlockSpec((1, D), lambda i: (0, 0)),
            pl.BlockSpec((D, n_cols), lambda i: (0, 0)),
            pl.BlockSpec((tm, LANES), lambda i: (i % spt, 0)),
            pl.BlockSpec((tm, LANES), lambda i: (i % spt, 0)),
        ],
        out_specs=[
            pl.BlockSpec((tm, qk_cols), lambda i: (i, 0)),
            pl.BlockSpec((tm, v_cols), lambda i: (i, 0)),
            pl.BlockSpec((tm, c_cols), lambda i: (i, 0)),
        ],
        out_shape=[
            jax.ShapeDtypeStruct((T, qk_cols), BF16),
            jax.ShapeDtypeStruct((T, v_cols), BF16),
            jax.ShapeDtypeStruct((T, c_cols), F32),
        ],
        compiler_params=_params(("parallel",)),
        name="in_proj",
    )(x2, g, w_perm, cos_t, sin_t)


def _attn_kernel(q_ref, k_ref, v_ref, lam_ref, g_ref, o_ref, m_sc, l_sc, acc_sc, *, tq, lambda_init):
    i = pl.program_id(2)
    q = q_ref[...]
    lane = lax.broadcasted_iota(jnp.int32, (1, LANES), 1)
    map0 = (lane // (ATTN_HEAD_DIM // 2)) % 2 == 0
    zero = jnp.zeros_like(q)
    qs = jnp.concatenate([jnp.where(map0, q, zero), jnp.where(map0, zero, q)], axis=0)

    m_sc[...] = jnp.full(m_sc.shape, NEG_INF, F32)
    l_sc[...] = jnp.zeros(l_sc.shape, F32)
    acc_sc[...] = jnp.zeros(acc_sc.shape, F32)

    def step(start, masked):
        k = k_ref[pl.ds(start, tq), :]
        v = v_ref[pl.ds(start, tq), :]
        s = lax.dot_general(qs, k, (((1,), (1,)), ((), ())), preferred_element_type=F32)
        if masked:
            row = lax.broadcasted_iota(jnp.int32, (2 * tq, tq), 0)
            col = lax.broadcasted_iota(jnp.int32, (2 * tq, tq), 1)
            s = jnp.where(col <= row % tq, s, NEG_INF)
        m_prev = m_sc[...]
        m_new = jnp.maximum(m_prev, jnp.max(s, axis=1, keepdims=True))
        alpha = jnp.exp(m_prev - m_new)
        p = jnp.exp(s - jnp.tile(m_new, (1, tq // LANES)))
        l_sc[...] = alpha * l_sc[...] + jnp.sum(p, axis=1, keepdims=True)
        acc_sc[...] = alpha * acc_sc[...] + jnp.dot(p.astype(BF16), v, preferred_element_type=F32)
        m_sc[...] = m_new

    def body(j, carry):
        step(pl.multiple_of(j * tq, tq), False)
        return carry

    lax.fori_loop(0, i, body, 0)
    step(pl.multiple_of(i * tq, tq), True)

    lv = lam_ref[...]
    lam = (jnp.exp(jnp.sum(lv[0:1] * lv[1:2], axis=1, keepdims=True))
           - jnp.exp(jnp.sum(lv[2:3] * lv[3:4], axis=1, keepdims=True)) + lambda_init)
    o = acc_sc[0:tq, :] / l_sc[0:tq, :] - lam * (acc_sc[tq:, :] / l_sc[tq:, :])
    o_ref[...] = (_rms(o, g_ref[...]) * (1.0 - lambda_init)).astype(o_ref.dtype)


def _diff_attn(qk, v, lamv, subln_g, batch, seq, tq, lambda_init):
    T = qk.shape[0]
    H = N_ATTN_HEADS
    nq = seq // tq
    return pl.pallas_call(
        functools.partial(_attn_kernel, tq=tq, lambda_init=lambda_init),
        grid=(batch, H, nq),
        in_specs=[
            pl.BlockSpec((tq, LANES), lambda b, h, i: (b * nq + i, h)),
            pl.BlockSpec((seq, LANES), lambda b, h, i: (b, H + h)),
            pl.BlockSpec((seq, LANES), lambda b, h, i: (b, h)),
            pl.BlockSpec((4, ATTN_HEAD_DIM), lambda b, h, i: (0, 0)),
            pl.BlockSpec((1, LANES), lambda b, h, i: (0, 0)),
        ],
        out_specs=pl.BlockSpec((tq, LANES), lambda b, h, i: (b * nq + i, h)),
        out_shape=jax.ShapeDtypeStruct((T, H * LANES), BF16),
        scratch_shapes=[
            pltpu.VMEM((2 * tq, LANES), F32),
            pltpu.VMEM((2 * tq, LANES), F32),
            pltpu.VMEM((2 * tq, LANES), F32),
        ],
        compiler_params=_params(("parallel", "parallel", "arbitrary")),
        name="diff_attn",
    )(qk, qk, v, lamv, subln_g)


def _conv_kernel(cur_ref, prev_ref, w_ref, b_ref, g_ref, beta_ref, o_ref, u_sc, *, ts, tiles_per_seq, width, rows):
    i = pl.program_id(0)
    first = (i % tiles_per_seq) == 0
    pc = prev_ref[...]
    up = pc[:, :width] * jax.nn.sigmoid(pc[:, width:])
    u_sc[0:CONV_HALO, :] = jnp.where(first, jnp.zeros_like(up), up)
    c = cur_ref[...]
    u_sc[CONV_HALO:, :] = c[:, :width] * jax.nn.sigmoid(c[:, width:])
    bias = b_ref[...]
    gam = g_ref[...]
    beta = beta_ref[...]
    shift = CONV_HALO - (CONV_KERNEL - 1)

    def chunk(r, carry):
        r0 = pl.multiple_of(r * rows, rows)
        acc = jnp.broadcast_to(bias, (rows, width))
        win = u_sc[pl.ds(r0, rows + CONV_HALO), :]
        for b in range(8):
            taps = [j for j in range(CONV_KERNEL) if (shift + j) % 8 == b]
            if not taps:
                continue
            hi = max(shift + j for j in taps) - b
            wb = win[b:b + hi + rows, :]
            for j in taps:
                a8 = shift + j - b
                acc = acc + w_ref[j:j + 1, :] * wb[a8:a8 + rows, :]
        mu = jnp.mean(acc, axis=-1, keepdims=True)
        d = acc - mu
        var = jnp.mean(d * d, axis=-1, keepdims=True)
        y = d * lax.rsqrt(var + EPS) * gam + beta
        o_ref[pl.ds(r0, rows), :] = (y * jax.nn.sigmoid(y)).astype(o_ref.dtype)
        return carry

    lax.fori_loop(0, ts // rows, chunk, 0)


def _conv_module(cvg, w, b, ln_g, ln_b, seq, ts):
    T, two_w = cvg.shape
    width = two_w // 2
    rows = 32
    hb = ts // CONV_HALO
    return pl.pallas_call(
        functools.partial(_conv_kernel, ts=ts, tiles_per_seq=seq // ts, width=width, rows=rows),
        grid=(T // ts,),
        in_specs=[
            pl.BlockSpec((ts, two_w), lambda i: (i, 0)),
            pl.BlockSpec((CONV_HALO, two_w), lambda i: (jnp.maximum(i * hb - 1, 0), 0)),
            pl.BlockSpec((CONV_KERNEL, width), lambda i: (0, 0)),
            pl.BlockSpec((1, width), lambda i: (0, 0)),
            pl.BlockSpec((1, width), lambda i: (0, 0)),
            pl.BlockSpec((1, width), lambda i: (0, 0)),
        ],
        out_specs=pl.BlockSpec((ts, width), lambda i: (i, 0)),
        out_shape=jax.ShapeDtypeStruct((T, width), BF16),
        scratch_shapes=[pltpu.VMEM((ts + CONV_HALO, width), F32)],
        compiler_params=_params(("parallel",)),
        name="conv_module",
    )(cvg, cvg, w, b, ln_g, ln_b)


def _top_rows(s, n_pick, payload=None):
    n = s.shape[0]
    iota = lax.broadcasted_iota(jnp.int32, s.shape, 0).astype(F32)
    vals, picks = [], []
    for _ in range(n_pick):
        mx = jnp.max(s, axis=0, keepdims=True)
        pos = jnp.min(jnp.where(s == mx, iota, float(n)), axis=0, keepdims=True)
        sel = iota == pos
        vals.append(mx)
        if payload is None:
            picks.append(pos.astype(jnp.int32))
        else:
            picks.append(jnp.sum(jnp.where(sel, payload, 0), axis=0, keepdims=True))
        s = jnp.where(sel, NEG_INF, s)
    return vals, picks


def _route_kernel(x_ref, at_ref, cv_ref, wo_ref, g_ref, wq_ref, keys_ref, h_ref, m_ref, e_ref, gt_ref, *, aw):
    h1 = (x_ref[...]
          + jnp.dot(at_ref[...], wo_ref[0:aw, :], preferred_element_type=F32)
          + jnp.dot(cv_ref[...], wo_ref[aw:, :], preferred_element_type=F32))
    h_ref[...] = h1
    m = _rms(h1, g_ref[...])
    m_ref[...] = m
    mb = m.astype(BF16)
    pairs = [(i, j) for i in range(PEER_TOPK) for j in range(PEER_TOPK) if (i + 1) * (j + 1) <= PEER_TOPK]
    n_pad = -len(pairs) % 8
    experts, gates = [], []
    for hd in range(N_PEER_HEADS):
        sv, si = [], []
        for c in range(2):
            c0 = (hd * 2 + c) * PEER_HALF
            q = jnp.dot(mb, wq_ref[:, c0:c0 + PEER_HALF], preferred_element_type=F32).astype(BF16)
            st = lax.dot_general(keys_ref[c], q, (((1,), (1,)), ((), ())), preferred_element_type=F32)
            v_, i_ = _top_rows(st, PEER_TOPK)
            sv.append(v_)
            si.append(i_)
        tm = sv[0][0].shape[1]
        cand = jnp.concatenate([sv[0][i] + sv[1][j] for i, j in pairs]
                               + [jnp.full((n_pad, tm), NEG_INF, F32)], axis=0)
        cexp = jnp.concatenate([si[0][i] * N_KEYS + si[1][j] for i, j in pairs]
                               + [jnp.zeros((n_pad, tm), jnp.int32)], axis=0)
        cs, ce = _top_rows(cand, PEER_TOPK, payload=cexp)
        ex = [jnp.exp(c_ - cs[0]) for c_ in cs]
        den = ex[0]
        for e_ in ex[1:]:
            den = den + e_
        experts += ce
        gates += [e_ / den for e_ in ex]
    e_ref[...] = jnp.concatenate(experts, axis=0).T
    gt_ref[...] = jnp.concatenate(gates, axis=0)


def _route(x2, attn, conv, w_out, g, wq, keys, tm):
    T, D = x2.shape
    aw = attn.shape[1]
    cw = conv.shape[1]
    qc = wq.shape[1]
    hk = N_PEER_HEADS * PEER_TOPK
    return pl.pallas_call(
        functools.partial(_route_kernel, aw=aw),
        grid=(T // tm,),
        in_specs=[
            pl.BlockSpec((tm, D), lambda i: (i, 0)),
            pl.BlockSpec((tm, aw), lambda i: (i, 0)),
            pl.BlockSpec((tm, cw), lambda i: (i, 0)),
            pl.BlockSpec((aw + cw, D), lambda i: (0, 0)),
            pl.BlockSpec((1, D), lambda i: (0, 0)),
            pl.BlockSpec((D, qc), lambda i: (0, 0)),
            pl.BlockSpec((2, N_KEYS, PEER_HALF), lambda i: (0, 0, 0)),
        ],
        out_specs=[
            pl.BlockSpec((tm, D), lambda i: (i, 0)),
            pl.BlockSpec((tm, D), lambda i: (i, 0)),
            pl.BlockSpec((tm, hk), lambda i: (i, 0)),
            pl.BlockSpec((hk, tm), lambda i: (0, i)),
        ],
        out_shape=[
            jax.ShapeDtypeStruct((T, D), F32),
            jax.ShapeDtypeStruct((T, D), F32),
            jax.ShapeDtypeStruct((T, hk), jnp.int32),
            jax.ShapeDtypeStruct((hk, T), F32),
        ],
        compiler_params=_params(("parallel",)),
        name="route",
    )(x2, attn, conv, w_out, g, wq, keys)


def _table_rows(tab):
    n, d = tab.shape
    return tab.astype(BF16).reshape(n, d // LANES, LANES)


def _peer_a_kernel(idx_ref, m_ref, gt_ref, tab_ref, h_ref, *scratch, tb, hk, sub):
    *p_parts, g_sc = scratch
    rows_per_part = hk // len(p_parts)
    lane = lax.broadcasted_iota(jnp.int32, (hk, tb), 1)

    def place(a_t, t):
        return jnp.where(lane == t, jnp.sum(g_sc[t], axis=1, keepdims=True), a_t)

    g_sc[0] = jnp.zeros(g_sc.shape[1:], F32)

    def token(t, a_t):
        a_t = place(a_t, jnp.maximum(t - 1, 0))
        x = m_ref[pl.ds(t, 1), :].reshape(sub, LANES)
        ids = idx_ref.at[t]

        def sublane_sums(q):
            p_sc = p_parts[q]
            g = p_sc[pl.ds(0, rows_per_part, stride=sub), :]
            for s in range(1, sub):
                g = g + p_sc[pl.ds(s, rows_per_part, stride=sub), :]
            g_sc[t, q * rows_per_part:(q + 1) * rows_per_part, :] = g

        for q, p_sc in enumerate(p_parts):
            for r in range(rows_per_part):
                p_sc[r * sub:(r + 1) * sub, :] = tab_ref[ids[q * rows_per_part + r]].astype(F32) * x
            if q > 0:
                sublane_sums(q - 1)
        sublane_sums(len(p_parts) - 1)
        return a_t

    a = place(lax.fori_loop(0, tb, token, jnp.zeros((hk, tb), F32)), tb - 1)
    h_ref[...] = (0.5 * a * (1.0 + lax.erf(a * (2.0 ** -0.5))) * gt_ref[...]).T


PEER_A_PARTS = 4


def _peer_a(idx, m, gates, tab, tb, start, tc):
    hk = idx.shape[1]
    sub = tab.shape[1]
    b0 = start // tb
    return pl.pallas_call(
        functools.partial(_peer_a_kernel, tb=tb, hk=hk, sub=sub),
        grid=(tc // tb,),
        in_specs=[
            pl.BlockSpec((tb, hk), lambda i: (b0 + i, 0), memory_space=pltpu.SMEM),
            pl.BlockSpec((tb, sub * LANES), lambda i: (b0 + i, 0)),
            pl.BlockSpec((hk, tb), lambda i: (0, b0 + i)),
            pl.BlockSpec(tab.shape, lambda i: (0, 0, 0), pipeline_mode=pl.Buffered(1)),
        ],
        out_specs=pl.BlockSpec((tb, hk), lambda i: (i, 0)),
        out_shape=jax.ShapeDtypeStruct((tc, hk), F32),
        scratch_shapes=[pltpu.VMEM((hk // PEER_A_PARTS * sub, LANES), F32)] * PEER_A_PARTS
        + [pltpu.VMEM((tb, hk, LANES), F32)],
        compiler_params=_params(("arbitrary",)),
        name="peer_a",
    )(idx, m, gates, tab)


SC_LANES = 16
SC_WORD_PAIR = 32
SC_ROW_BUFS = 4
SC_LOAD_GROUP = 4
SC_ROW_GROUP = 4


def _table_words(tab):
    n, d = tab.shape
    t = tab.astype(BF16).reshape(n, d // SC_WORD_PAIR, 2, SC_LANES).transpose(0, 1, 3, 2)
    return lax.bitcast_convert_type(t.reshape(n, d // 2, 2), jnp.int32)


def _peer_y_sc(idx, h, tab_words, start, tc):
    hk = idx.shape[1]
    dw = tab_words.shape[1]
    d = 2 * dw
    info = plsc.get_sparse_core_info()
    nc, ns = info.num_cores, info.num_subcores
    tpw = tc // (nc * ns)
    rq = hk // SC_ROW_BUFS
    n_vec = dw // SC_LANES
    assert tc % (2 * nc * ns) == 0 and n_vec % SC_LOAD_GROUP == 0 and rq % SC_ROW_GROUP == 0
    mesh = plsc.VectorSubcoreMesh(core_axis_name="c", subcore_axis_name="s")

    @functools.partial(
        pl.kernel, mesh=mesh,
        out_type=jax.ShapeDtypeStruct((tc, d), F32),
        scratch_types=[pltpu.VMEM((tpw, hk), jnp.int32), pltpu.VMEM((tpw, hk), F32)]
        + [pltpu.VMEM((rq, dw), jnp.int32)] * SC_ROW_BUFS
        + [pltpu.VMEM((d,), F32)] * 2
        + [pltpu.SemaphoreType.DMA] * (SC_ROW_BUFS + 2),
        compiler_params=pltpu.CompilerParams(needs_layout_passes=False),
        name="peer_y_sc",
    )
    def body(idx_hbm, h_hbm, tab_hbm, y_hbm, ids_v, w_v, *scratch):
        rows = scratch[:SC_ROW_BUFS]
        accs = scratch[SC_ROW_BUFS:SC_ROW_BUFS + 2]
        row_sems = scratch[SC_ROW_BUFS + 2:2 * SC_ROW_BUFS + 2]
        out_sems = scratch[2 * SC_ROW_BUFS + 2:]
        base = (lax.axis_index("s") * nc + lax.axis_index("c")) * tpw
        pltpu.sync_copy(idx_hbm.at[pl.ds(start + base, tpw)], ids_v)
        pltpu.sync_copy(h_hbm.at[pl.ds(base, tpw)], w_v)

        def gather(i, q):
            return pltpu.make_async_copy(tab_hbm.at[ids_v.at[i, pl.ds(q * rq, rq)]], rows[q], row_sems[q])

        def put(i, par):
            return pltpu.make_async_copy(accs[par], y_hbm.at[base + i], out_sems[par])

        def accumulate(i, q, acc):
            lanes0 = jnp.zeros((SC_LANES,), jnp.int32)

            n_groups = n_vec // SC_LOAD_GROUP

            @pl.loop(0, rq, step=SC_ROW_GROUP)
            def _(r0):
                ws = [plsc.load_gather(w_v, [lanes0 + i, lanes0 + (q * rq + r0 + j)]) for j in range(SC_ROW_GROUP)]

                def load(g):
                    return [[rows[q][r0 + j, pl.ds((g * SC_LOAD_GROUP + c) * SC_LANES, SC_LANES)]
                             for c in range(SC_LOAD_GROUP)] for j in range(SC_ROW_GROUP)]

                words = load(0)
                for g in range(n_groups):
                    nxt = load(g + 1) if g + 1 < n_groups else None
                    for c in range(SC_LOAD_GROUP):
                        lo_sum = hi_sum = None
                        for j in range(SC_ROW_GROUP):
                            lo, hi = plsc.unpack(plsc.bitcast(words[j][c], BF16), format=plsc.PackFormat.INTERLEAVED,
                                                 preferred_element_type=F32)
                            lo_sum = ws[j] * lo if lo_sum is None else lo_sum + ws[j] * lo
                            hi_sum = ws[j] * hi if hi_sum is None else hi_sum + ws[j] * hi
                        col = (g * SC_LOAD_GROUP + c) * SC_WORD_PAIR
                        plsc.addupdate(acc.at[pl.ds(col, SC_LANES)], lo_sum)
                        plsc.addupdate(acc.at[pl.ds(col + SC_LANES, SC_LANES)], hi_sum)
                    words = nxt

        for q in range(SC_ROW_BUFS):
            gather(0, q).start()

        @pl.loop(0, tpw, step=2)
        def _(i0):
            for par in range(2):
                i = i0 + par

                @pl.when(i0 > 0)
                def _():
                    put(i - 2, par).wait()

                zero = jnp.zeros((SC_LANES,), F32)
                for c in range(d // SC_LANES):
                    accs[par][pl.ds(c * SC_LANES, SC_LANES)] = zero
                for q in range(SC_ROW_BUFS):
                    gather(i, q).wait()
                    accumulate(i, q, accs[par])

                    @pl.when(i + 1 < tpw)
                    def _():
                        gather(i + 1, q).start()

                put(i, par).start()

        for par in range(2):
            put(tpw - 2 + par, par).wait()

    return body(idx, h, tab_words)


def _final_kernel(h_ref, y_ref, p_ref, wp_ref, gp_ref, wg_ref, gf_ref, *rest, apply_final):
    o_ref = rest[-1]
    h2 = h_ref[...] + y_ref[...]
    e = jnp.dot(p_ref[...].astype(BF16), wp_ref[...], preferred_element_type=F32)
    n = _rms(h2, gp_ref[...]).astype(BF16)
    gate = jax.nn.sigmoid(jnp.dot(n, wg_ref[...], preferred_element_type=F32))
    h3 = h2 + e * gate
    o_ref[...] = _rms(h3, gf_ref[...]) if apply_final else h3


def _ple(h1, y_chunk, p2, w_proj, g_ple, w_gate, g_final, apply_final, tm, start, out_prev):
    T, D = h1.shape
    size = y_chunk.shape[0]
    pd = p2.shape[1]
    b0 = start // tm
    in_specs = [
        pl.BlockSpec((tm, D), lambda i: (b0 + i, 0)),
        pl.BlockSpec((tm, D), lambda i: (i, 0)),
        pl.BlockSpec((tm, pd), lambda i: (b0 + i, 0)),
        pl.BlockSpec((pd, D), lambda i: (0, 0)),
        pl.BlockSpec((1, D), lambda i: (0, 0)),
        pl.BlockSpec((D, D), lambda i: (0, 0)),
        pl.BlockSpec((1, D), lambda i: (0, 0)),
    ]
    args = [h1, y_chunk, p2, w_proj, g_ple, w_gate, g_final]
    aliases = {}
    if out_prev is not None:
        in_specs.append(pl.BlockSpec(memory_space=pl.ANY))
        args.append(out_prev)
        aliases = {len(args) - 1: 0}
    return pl.pallas_call(
        functools.partial(_final_kernel, apply_final=apply_final),
        grid=(size // tm,),
        in_specs=in_specs,
        out_specs=pl.BlockSpec((tm, D), lambda i: (b0 + i, 0)),
        out_shape=jax.ShapeDtypeStruct((T, D), F32),
        input_output_aliases=aliases,
        compiler_params=_params(("parallel",)),
        name="ple_final",
    )(*args)


def _qk_permutation():
    half = ATTN_HEAD_DIM // 2
    perm = np.zeros(N_ATTN_HEADS * LANES, np.int32)
    for h in range(N_ATTN_HEADS):
        for c in range(2):
            for d in range(ATTN_HEAD_DIM):
                perm[h * LANES + (d // half) * 2 * half + c * half + d % half] = h * LANES + c * ATTN_HEAD_DIM + d
    return perm


def _rope_tables(seq):
    half = ATTN_HEAD_DIM // 2
    inv_freq = 1.0 / (ROPE_THETA ** (jnp.arange(half, dtype=F32) * 2.0 / ATTN_HEAD_DIM))
    ang = jnp.arange(seq).astype(F32)[:, None] * inv_freq[None, :]
    cos, sin = jnp.cos(ang), jnp.sin(ang)
    return jnp.tile(cos, (1, 4)), jnp.concatenate([-sin, -sin, sin, sin], axis=1)


def _token_chunks(total, big):
    sizes = [big] * (total // big)
    if big % 1024 == 0:
        sizes[-1:] = [big // 2, big // 4, big // 4]
    return sizes


def kernel(x, p, attn_norm_g, w_in, lambda_q1, lambda_k1, lambda_q2, lambda_k2, subln_g, conv_w, conv_b,
           conv_ln_g, conv_ln_b, w_out, ffn_norm_g, peer_wq, peer_keys, peer_u, peer_v, ple_norm_g,
           ple_w_gate, ple_w_proj, final_norm_g):
    B, S, D = x.shape
    T = B * S
    depth = w_in.shape[0]
    assert depth >= 1
    tm = min(512, S)
    tq = min(512, S)
    tr = min(256, S)
    tb = min(256, T)
    tc = min(4096, T)
    qk_half = N_ATTN_HEADS * 2 * ATTN_HEAD_DIM

    perm = _qk_permutation()
    col_order = np.concatenate([perm, qk_half + perm, np.arange(2 * qk_half, w_in.shape[2])])
    cos_t, sin_t = _rope_tables(S)
    row = lambda a: a.reshape(1, -1).astype(F32)

    h = x.reshape(T, D)
    for l in range(depth):
        lambda_init = 0.8 - 0.6 * math.exp(-0.3 * l)
        w_perm = w_in[l][:, col_order].astype(BF16)
        qk, v, cvg = _in_proj(h, row(attn_norm_g[l]), w_perm, cos_t, sin_t, S, tm)
        lamv = jnp.stack([lambda_q1[l], lambda_k1[l], lambda_q2[l], lambda_k2[l]]).astype(F32)
        attn = _diff_attn(qk, v, lamv, row(subln_g[l]), B, S, tq, lambda_init)
        conv = _conv_module(cvg, conv_w[l], row(conv_b[l]), row(conv_ln_g[l]), row(conv_ln_b[l]), S, tm)
        h1, m, experts, gates = _route(h, attn, conv, w_out[l].astype(BF16), row(ffn_norm_g[l]),
                                       peer_wq[l].astype(BF16), peer_keys[l].astype(BF16), tr)
        u_rows, v_words = _table_rows(peer_u[l]), _table_words(peer_v[l])
        ple_args = (p[l].reshape(T, -1), ple_w_proj[l].astype(BF16), row(ple_norm_g[l]), ple_w_gate[l].astype(BF16),
                    row(final_norm_g), l == depth - 1, tm)
        sizes = _token_chunks(T, tc)
        assert sum(sizes) == T and all(size % tm == 0 and size % tb == 0 for size in sizes), (T, sizes)
        ys, start = [], 0
        for size in sizes:
            hw = _peer_a(experts, m, gates, u_rows, tb, start, size)
            ys.append((start, _peer_y_sc(experts, hw, v_words, start, size)))
            start += size
        h = None
        for start, y_chunk in ys:
            h = _ple(h1, y_chunk, *ple_args, start, h)
    return h.reshape(B, S, D)
```

```python
import functools
import math

import numpy as np
import jax
import jax.numpy as jnp
from jax import lax
from jax.experimental import pallas as pl
from jax.experimental.pallas import tpu as pltpu
from jax.experimental.pallas import tpu_sc as plsc

EPS = 1e-6
ROPE_THETA = 10000.0
ATTN_HEAD_DIM = 64
N_ATTN_HEADS = 4
CONV_KERNEL = 31
N_PEER_HEADS = 8
N_KEYS = 128
PEER_TOPK = 16
PEER_HALF = 128

LANES = 128
CONV_HALO = 32
VMEM_LIMIT = 56 * 1024 * 1024

F32 = jnp.float32
BF16 = jnp.bfloat16
NEG_INF = float("-inf")


def _rms(x, g):
    ms = jnp.mean(x * x, axis=-1, keepdims=True)
    return x * lax.rsqrt(ms + EPS) * g


def _params(sem):
    return pltpu.CompilerParams(dimension_semantics=sem, vmem_limit_bytes=VMEM_LIMIT)


def _inproj_kernel(x_ref, g_ref, w_ref, cos_ref, sin_ref, qk_ref, v_ref, c_ref, *, qk_cols, v_cols):
    a = _rms(x_ref[...], g_ref[...]).astype(BF16)
    cos = cos_ref[...]
    sin = sin_ref[...]
    for col in range(0, qk_cols, 2 * LANES):
        z = jnp.dot(a, w_ref[:, col:col + 2 * LANES], preferred_element_type=F32)
        for half in range(2):
            zz = z[:, half * LANES:(half + 1) * LANES]
            r = zz * cos + pltpu.roll(zz, LANES // 2, 1) * sin
            c0 = col + half * LANES
            if c0 < qk_cols // 2:
                r = r * (ATTN_HEAD_DIM ** -0.5)
            qk_ref[:, c0:c0 + LANES] = r.astype(BF16)
    v_ref[...] = jnp.dot(a, w_ref[:, qk_cols:qk_cols + v_cols], preferred_element_type=F32).astype(BF16)
    c_ref[...] = jnp.dot(a, w_ref[:, qk_cols + v_cols:], preferred_element_type=F32)


def _in_proj(x2, g, w_perm, cos_t, sin_t, seq, tm):
    T, D = x2.shape
    n_cols = w_perm.shape[1]
    qk_cols = 2 * N_ATTN_HEADS * 2 * ATTN_HEAD_DIM
    v_cols = N_ATTN_HEADS * 2 * ATTN_HEAD_DIM
    c_cols = n_cols - qk_cols - v_cols
    spt = seq // tm
    return pl.pallas_call(
        functools.partial(_inproj_kernel, qk_cols=qk_cols, v_cols=v_cols),
        grid=(T // tm,),
        in_specs=[
            pl.BlockSpec((tm, D), lambda i: (i, 0)),
            pl.BlockSpec((1, D), lambda i: (0, 0)),
            pl.BlockSpec((D, n_cols), lambda i: (0, 0)),
            pl.BlockSpec((tm, LANES), lambda i: (i % spt, 0)),
            pl.BlockSpec((tm, LANES), lambda i: (i % spt, 0)),
        ],
        out_specs=[
            pl.BlockSpec((tm, qk_cols), lambda i: (i, 0)),
            pl.BlockSpec((tm, v_cols), lambda i: (i, 0)),
            pl.BlockSpec((tm, c_cols), lambda i: (i, 0)),
        ],
        out_shape=[
            jax.ShapeDtypeStruct((T, qk_cols), BF16),
            jax.ShapeDtypeStruct((T, v_cols), BF16),
            jax.ShapeDtypeStruct((T, c_cols), F32),
        ],
        compiler_params=_params(("parallel",)),
        name="in_proj",
    )(x2, g, w_perm, cos_t, sin_t)


def _attn_kernel(q_ref, k_ref, v_ref, lam_ref, g_ref, o_ref, m_sc, l_sc, acc_sc, *, tq, lambda_init):
    i = pl.program_id(2)
    q = q_ref[...]
    lane = lax.broadcasted_iota(jnp.int32, (1, LANES), 1)
    map0 = (lane // (ATTN_HEAD_DIM // 2)) % 2 == 0
    zero = jnp.zeros_like(q)
    qs = jnp.concatenate([jnp.where(map0, q, zero), jnp.where(map0, zero, q)], axis=0)

    m_sc[...] = jnp.full(m_sc.shape, NEG_INF, F32)
    l_sc[...] = jnp.zeros(l_sc.shape, F32)
    acc_sc[...] = jnp.zeros(acc_sc.shape, F32)

    def step(start, masked):
        k = k_ref[pl.ds(start, tq), :]
        v = v_ref[pl.ds(start, tq), :]
        s = lax.dot_general(qs, k, (((1,), (1,)), ((), ())), preferred_element_type=F32)
        if masked:
            row = lax.broadcasted_iota(jnp.int32, (2 * tq, tq), 0)
            col = lax.broadcasted_iota(jnp.int32, (2 * tq, tq), 1)
            s = jnp.where(col <= row % tq, s, NEG_INF)
        m_prev = m_sc[...]
        m_new = jnp.maximum(m_prev, jnp.max(s, axis=1, keepdims=True))
        alpha = jnp.exp(m_prev - m_new)
        p = jnp.exp(s - jnp.tile(m_new, (1, tq // LANES)))
        l_sc[...] = alpha * l_sc[...] + jnp.sum(p, axis=1, keepdims=True)
        acc_sc[...] = alpha * acc_sc[...] + jnp.dot(p.astype(BF16), v, preferred_element_type=F32)
        m_sc[...] = m_new

    def body(j, carry):
        step(pl.multiple_of(j * tq, tq), False)
        return carry

    lax.fori_loop(0, i, body, 0)
    step(pl.multiple_of(i * tq, tq), True)

    lv = lam_ref[...]
    lam = (jnp.exp(jnp.sum(lv[0:1] * lv[1:2], axis=1, keepdims=True))
           - jnp.exp(jnp.sum(lv[2:3] * lv[3:4], axis=1, keepdims=True)) + lambda_init)
    o = acc_sc[0:tq, :] / l_sc[0:tq, :] - lam * (acc_sc[tq:, :] / l_sc[tq:, :])
    o_ref[...] = (_rms(o, g_ref[...]) * (1.0 - lambda_init)).astype(o_ref.dtype)


def _diff_attn(qk, v, lamv, subln_g, batch, seq, tq, lambda_init):
    T = qk.shape[0]
    H = N_ATTN_HEADS
    nq = seq // tq
    return pl.pallas_call(
        functools.partial(_attn_kernel, tq=tq, lambda_init=lambda_init),
        grid=(batch, H, nq),
        in_specs=[
            pl.BlockSpec((tq, LANES), lambda b, h, i: (b * nq + i, h)),
            pl.BlockSpec((seq, LANES), lambda b, h, i: (b, H + h)),
            pl.BlockSpec((seq, LANES), lambda b, h, i: (b, h)),
            pl.BlockSpec((4, ATTN_HEAD_DIM), lambda b, h, i: (0, 0)),
            pl.BlockSpec((1, LANES), lambda b, h, i: (0, 0)),
        ],
        out_specs=pl.BlockSpec((tq, LANES), lambda b, h, i: (b * nq + i, h)),
        out_shape=jax.ShapeDtypeStruct((T, H * LANES), BF16),
        scratch_shapes=[
            pltpu.VMEM((2 * tq, LANES), F32),
            pltpu.VMEM((2 * tq, LANES), F32),
            pltpu.VMEM((2 * tq, LANES), F32),
        ],
        compiler_params=_params(("parallel", "parallel", "arbitrary")),
        name="diff_attn",
    )(qk, qk, v, lamv, subln_g)


def _conv_kernel(cur_ref, prev_ref, w_ref, b_ref, g_ref, beta_ref, o_ref, u_sc, *, ts, tiles_per_seq, width, rows):
    i = pl.program_id(0)
    first = (i % tiles_per_seq) == 0
    pc = prev_ref[...]
    up = pc[:, :width] * jax.nn.sigmoid(pc[:, width:])
    u_sc[0:CONV_HALO, :] = jnp.where(first, jnp.zeros_like(up), up)
    c = cur_ref[...]
    u_sc[CONV_HALO:, :] = c[:, :width] * jax.nn.sigmoid(c[:, width:])
    bias = b_ref[...]
    gam = g_ref[...]
    beta = beta_ref[...]
    shift = CONV_HALO - (CONV_KERNEL - 1)

    def chunk(r, carry):
        r0 = pl.multiple_of(r * rows, rows)
        acc = jnp.broadcast_to(bias, (rows, width))
        win = u_sc[pl.ds(r0, rows + CONV_HALO), :]
        for b in range(8):
            taps = [j for j in range(CONV_KERNEL) if (shift + j) % 8 == b]
            if not taps:
                continue
            wb = win if b == 0 else pltpu.roll(win, rows + CONV_HALO - b, 0)
            for j in taps:
                a8 = shift + j - b
                acc = acc + w_ref[j:j + 1, :] * wb[a8:a8 + rows, :]
        mu = jnp.mean(acc, axis=-1, keepdims=True)
        d = acc - mu
        var = jnp.mean(d * d, axis=-1, keepdims=True)
        y = d * lax.rsqrt(var + EPS) * gam + beta
        o_ref[pl.ds(r0, rows), :] = (y * jax.nn.sigmoid(y)).astype(o_ref.dtype)
        return carry

    lax.fori_loop(0, ts // rows, chunk, 0)


def _conv_module(cvg, w, b, ln_g, ln_b, seq, ts):
    T, two_w = cvg.shape
    width = two_w // 2
    rows = 32
    hb = ts // CONV_HALO
    return pl.pallas_call(
        functools.partial(_conv_kernel, ts=ts, tiles_per_seq=seq // ts, width=width, rows=rows),
        grid=(T // ts,),
        in_specs=[
            pl.BlockSpec((ts, two_w), lambda i: (i, 0)),
            pl.BlockSpec((CONV_HALO, two_w), lambda i: (jnp.maximum(i * hb - 1, 0), 0)),
            pl.BlockSpec((CONV_KERNEL, width), lambda i: (0, 0)),
            pl.BlockSpec((1, width), lambda i: (0, 0)),
            pl.BlockSpec((1, width), lambda i: (0, 0)),
            pl.BlockSpec((1, width), lambda i: (0, 0)),
        ],
        out_specs=pl.BlockSpec((ts, width), lambda i: (i, 0)),
        out_shape=jax.ShapeDtypeStruct((T, width), BF16),
        scratch_shapes=[pltpu.VMEM((ts + CONV_HALO, width), F32)],
        compiler_params=_params(("parallel",)),
        name="conv_module",
    )(cvg, cvg, w, b, ln_g, ln_b)


def _top_rows(s, n_pick, payload=None):
    n = s.shape[0]
    iota = lax.broadcasted_iota(jnp.int32, s.shape, 0).astype(F32)
    vals, picks = [], []
    for _ in range(n_pick):
        mx = jnp.max(s, axis=0, keepdims=True)
        pos = jnp.min(jnp.where(s == mx, iota, float(n)), axis=0, keepdims=True)
        sel = iota == pos
        vals.append(mx)
        if payload is None:
            picks.append(pos.astype(jnp.int32))
        else:
            picks.append(jnp.sum(jnp.where(sel, payload, 0), axis=0, keepdims=True))
        s = jnp.where(sel, NEG_INF, s)
    return vals, picks


def _route_kernel(x_ref, at_ref, cv_ref, wo_ref, g_ref, wq_ref, keys_ref, h_ref, m_ref, e_ref, gt_ref, *, aw):
    h1 = (x_ref[...]
          + jnp.dot(at_ref[...], wo_ref[0:aw, :], preferred_element_type=F32)
          + jnp.dot(cv_ref[...], wo_ref[aw:, :], preferred_element_type=F32))
    h_ref[...] = h1
    m = _rms(h1, g_ref[...])
    m_ref[...] = m
    mb = m.astype(BF16)
    pairs = [(i, j) for i in range(PEER_TOPK) for j in range(PEER_TOPK) if (i + 1) * (j + 1) <= PEER_TOPK]
    n_pad = -len(pairs) % 8
    experts, gates = [], []
    for hd in range(N_PEER_HEADS):
        sv, si = [], []
        for c in range(2):
            c0 = (hd * 2 + c) * PEER_HALF
            q = jnp.dot(mb, wq_ref[:, c0:c0 + PEER_HALF], preferred_element_type=F32).astype(BF16)
            st = lax.dot_general(keys_ref[c], q, (((1,), (1,)), ((), ())), preferred_element_type=F32)
            v_, i_ = _top_rows(st, PEER_TOPK)
            sv.append(v_)
            si.append(i_)
        tm = sv[0][0].shape[1]
        cand = jnp.concatenate([sv[0][i] + sv[1][j] for i, j in pairs]
                               + [jnp.full((n_pad, tm), NEG_INF, F32)], axis=0)
        cexp = jnp.concatenate([si[0][i] * N_KEYS + si[1][j] for i, j in pairs]
                               + [jnp.zeros((n_pad, tm), jnp.int32)], axis=0)
        cs, ce = _top_rows(cand, PEER_TOPK, payload=cexp)
        ex = [jnp.exp(c_ - cs[0]) for c_ in cs]
        den = ex[0]
        for e_ in ex[1:]:
            den = den + e_
        experts += ce
        gates += [e_ / den for e_ in ex]
    e_ref[...] = jnp.concatenate(experts, axis=0).T
    gt_ref[...] = jnp.concatenate(gates, axis=0)


def _route(x2, attn, conv, w_out, g, wq, keys, tm):
    T, D = x2.shape
    aw = attn.shape[1]
    cw = conv.shape[1]
    qc = wq.shape[1]
    hk = N_PEER_HEADS * PEER_TOPK
    return pl.pallas_call(
        functools.partial(_route_kernel, aw=aw),
        grid=(T // tm,),
        in_specs=[
            pl.BlockSpec((tm, D), lambda i: (i, 0)),
            pl.BlockSpec((tm, aw), lambda i: (i, 0)),
            pl.BlockSpec((tm, cw), lambda i: (i, 0)),
            pl.BlockSpec((aw + cw, D), lambda i: (0, 0)),
            pl.BlockSpec((1, D), lambda i: (0, 0)),
            pl.BlockSpec((D, qc), lambda i: (0, 0)),
            pl.BlockSpec((2, N_KEYS, PEER_HALF), lambda i: (0, 0, 0)),
        ],
        out_specs=[
            pl.BlockSpec((tm, D), lambda i: (i, 0)),
            pl.BlockSpec((tm, D), lambda i: (i, 0)),
            pl.BlockSpec((tm, hk), lambda i: (i, 0)),
            pl.BlockSpec((hk, tm), lambda i: (0, i)),
        ],
        out_shape=[
            jax.ShapeDtypeStruct((T, D), F32),
            jax.ShapeDtypeStruct((T, D), F32),
            jax.ShapeDtypeStruct((T, hk), jnp.int32),
            jax.ShapeDtypeStruct((hk, T), F32),
        ],
        compiler_params=_params(("parallel",)),
        name="route",
    )(x2, attn, conv, w_out, g, wq, keys)


def _table_rows(tab):
    n, d = tab.shape
    return tab.astype(BF16).reshape(n, d // LANES, LANES)


def _peer_a_kernel(idx_ref, m_ref, gt_ref, tab_ref, h_ref, *scratch, tb, hk, sub):
    *p_parts, g_sc = scratch
    rows_per_part = hk // len(p_parts)
    lane = lax.broadcasted_iota(jnp.int32, (hk, tb), 1)

    def place(a_t, t):
        return jnp.where(lane == t, jnp.sum(g_sc[t], axis=1, keepdims=True), a_t)

    g_sc[0] = jnp.zeros(g_sc.shape[1:], F32)

    def token(t, a_t):
        a_t = place(a_t, jnp.maximum(t - 1, 0))
        x = m_ref[pl.ds(t, 1), :].reshape(sub, LANES)
        ids = idx_ref.at[t]

        def sublane_sums(q):
            p_sc = p_parts[q]
            g = p_sc[pl.ds(0, rows_per_part, stride=sub), :]
            for s in range(1, sub):
                g = g + p_sc[pl.ds(s, rows_per_part, stride=sub), :]
            g_sc[t, q * rows_per_part:(q + 1) * rows_per_part, :] = g

        for q, p_sc in enumerate(p_parts):
            for r in range(rows_per_part):
                p_sc[r * sub:(r + 1) * sub, :] = tab_ref[ids[q * rows_per_part + r]].astype(F32) * x
            if q > 0:
                sublane_sums(q - 1)
        sublane_sums(len(p_parts) - 1)
        return a_t

    a = place(lax.fori_loop(0, tb, token, jnp.zeros((hk, tb), F32)), tb - 1)
    h_ref[...] = (0.5 * a * (1.0 + lax.erf(a * (2.0 ** -0.5))) * gt_ref[...]).T


PEER_A_PARTS = 4


def _peer_a(idx, m, gates, tab, tb, start, tc):
    hk = idx.shape[1]
    sub = tab.shape[1]
    b0 = start // tb
    return pl.pallas_call(
        functools.partial(_peer_a_kernel, tb=tb, hk=hk, sub=sub),
        grid=(tc // tb,),
        in_specs=[
            pl.BlockSpec((tb, hk), lambda i: (b0 + i, 0), memory_space=pltpu.SMEM),
            pl.BlockSpec((tb, sub * LANES), lambda i: (b0 + i, 0)),
            pl.BlockSpec((hk, tb), lambda i: (0, b0 + i)),
            pl.BlockSpec(tab.shape, lambda i: (0, 0, 0), pipeline_mode=pl.Buffered(1)),
        ],
        out_specs=pl.BlockSpec((tb, hk), lambda i: (i, 0)),
        out_shape=jax.ShapeDtypeStruct((tc, hk), F32),
        scratch_shapes=[pltpu.VMEM((hk // PEER_A_PARTS * sub, LANES), F32)] * PEER_A_PARTS
        + [pltpu.VMEM((tb, hk, LANES), F32)],
        compiler_params=_params(("arbitrary",)),
        name="peer_a",
    )(idx, m, gates, tab)


SC_LANES = 16
SC_WORD_PAIR = 32
SC_ROW_BUFS = 4
SC_LOAD_GROUP = 4
SC_ROW_GROUP = 4


def _table_words(tab):
    n, d = tab.shape
    t = tab.astype(BF16).reshape(n, d // SC_WORD_PAIR, 2, SC_LANES).transpose(0, 1, 3, 2)
    return lax.bitcast_convert_type(t.reshape(n, d // 2, 2), jnp.int32)


def _peer_y_sc(idx, h, tab_words, start, tc):
    hk = idx.shape[1]
    dw = tab_words.shape[1]
    d = 2 * dw
    info = plsc.get_sparse_core_info()
    nc, ns = info.num_cores, info.num_subcores
    tpw = tc // (nc * ns)
    rq = hk // SC_ROW_BUFS
    n_vec = dw // SC_LANES
    assert tc % (2 * nc * ns) == 0 and n_vec % SC_LOAD_GROUP == 0 and rq % SC_ROW_GROUP == 0
    mesh = plsc.VectorSubcoreMesh(core_axis_name="c", subcore_axis_name="s")

    @functools.partial(
        pl.kernel, mesh=mesh,
        out_type=jax.ShapeDtypeStruct((tc, d), F32),
        scratch_types=[pltpu.VMEM((tpw, hk), jnp.int32), pltpu.VMEM((tpw, hk), F32)]
        + [pltpu.VMEM((rq, dw), jnp.int32)] * SC_ROW_BUFS
        + [pltpu.VMEM((d,), F32)] * 2
        + [pltpu.SemaphoreType.DMA] * (SC_ROW_BUFS + 2),
        compiler_params=pltpu.CompilerParams(needs_layout_passes=False),
        name="peer_y_sc",
    )
    def body(idx_hbm, h_hbm, tab_hbm, y_hbm, ids_v, w_v, *scratch):
        rows = scratch[:SC_ROW_BUFS]
        accs = scratch[SC_ROW_BUFS:SC_ROW_BUFS + 2]
        row_sems = scratch[SC_ROW_BUFS + 2:2 * SC_ROW_BUFS + 2]
        out_sems = scratch[2 * SC_ROW_BUFS + 2:]
        base = (lax.axis_index("s") * nc + lax.axis_index("c")) * tpw
        pltpu.sync_copy(idx_hbm.at[pl.ds(start + base, tpw)], ids_v)
        pltpu.sync_copy(h_hbm.at[pl.ds(base, tpw)], w_v)

        def gather(i, q):
            return pltpu.make_async_copy(tab_hbm.at[ids_v.at[i, pl.ds(q * rq, rq)]], rows[q], row_sems[q])

        def put(i, par):
            return pltpu.make_async_copy(accs[par], y_hbm.at[base + i], out_sems[par])

        def accumulate(i, q, acc):
            lanes0 = jnp.zeros((SC_LANES,), jnp.int32)

            n_groups = n_vec // SC_LOAD_GROUP

            @pl.loop(0, rq, step=SC_ROW_GROUP)
            def _(r0):
                ws = [plsc.load_gather(w_v, [lanes0 + i, lanes0 + (q * rq + r0 + j)]) for j in range(SC_ROW_GROUP)]

                def load(g):
                    return [[rows[q][r0 + j, pl.ds((g * SC_LOAD_GROUP + c) * SC_LANES, SC_LANES)]
                             for c in range(SC_LOAD_GROUP)] for j in range(SC_ROW_GROUP)]

                words = load(0)
                for g in range(n_groups):
                    nxt = load(g + 1) if g + 1 < n_groups else None
                    for c in range(SC_LOAD_GROUP):
                        lo_sum = hi_sum = None
                        for j in range(SC_ROW_GROUP):
                            lo, hi = plsc.unpack(plsc.bitcast(words[j][c], BF16), format=plsc.PackFormat.INTERLEAVED,
                                                 preferred_element_type=F32)
                            lo_sum = ws[j] * lo if lo_sum is None else lo_sum + ws[j] * lo
                            hi_sum = ws[j] * hi if hi_sum is None else hi_sum + ws[j] * hi
                        col = (g * SC_LOAD_GROUP + c) * SC_WORD_PAIR
                        plsc.addupdate(acc.at[pl.ds(col, SC_LANES)], lo_sum)
                        plsc.addupdate(acc.at[pl.ds(col + SC_LANES, SC_LANES)], hi_sum)
                    words = nxt

        for q in range(SC_ROW_BUFS):
            gather(0, q).start()

        @pl.loop(0, tpw, step=2)
        def _(i0):
            for par in range(2):
                i = i0 + par

                @pl.when(i0 > 0)
                def _():
                    put(i - 2, par).wait()

                zero = jnp.zeros((SC_LANES,), F32)
                for c in range(d // SC_LANES):
                    accs[par][pl.ds(c * SC_LANES, SC_LANES)] = zero
                for q in range(SC_ROW_BUFS):
                    gather(i, q).wait()
                    accumulate(i, q, accs[par])

                    @pl.when(i + 1 < tpw)
                    def _():
                        gather(i + 1, q).start()

                put(i, par).start()

        for par in range(2):
            put(tpw - 2 + par, par).wait()

    return body(idx, h, tab_words)


def _final_kernel(h_ref, y_ref, p_ref, wp_ref, gp_ref, wg_ref, gf_ref, *rest, apply_final):
    o_ref = rest[-1]
    h2 = h_ref[...] + y_ref[...]
    e = jnp.dot(p_ref[...].astype(BF16), wp_ref[...], preferred_element_type=F32)
    n = _rms(h2, gp_ref[...]).astype(BF16)
    gate = jax.nn.sigmoid(jnp.dot(n, wg_ref[...], preferred_element_type=F32))
    h3 = h2 + e * gate
    o_ref[...] = _rms(h3, gf_ref[...]) if apply_final else h3


def _ple(h1, y_chunk, p2, w_proj, g_ple, w_gate, g_final, apply_final, tm, start, out_prev):
    T, D = h1.shape
    size = y_chunk.shape[0]
    pd = p2.shape[1]
    b0 = start // tm
    in_specs = [
        pl.BlockSpec((tm, D), lambda i: (b0 + i, 0)),
        pl.BlockSpec((tm, D), lambda i: (i, 0)),
        pl.BlockSpec((tm, pd), lambda i: (b0 + i, 0)),
        pl.BlockSpec((pd, D), lambda i: (0, 0)),
        pl.BlockSpec((1, D), lambda i: (0, 0)),
        pl.BlockSpec((D, D), lambda i: (0, 0)),
        pl.BlockSpec((1, D), lambda i: (0, 0)),
    ]
    args = [h1, y_chunk, p2, w_proj, g_ple, w_gate, g_final]
    aliases = {}
    if out_prev is not None:
        in_specs.append(pl.BlockSpec(memory_space=pl.ANY))
        args.append(out_prev)
        aliases = {len(args) - 1: 0}
    return pl.pallas_call(
        functools.partial(_final_kernel, apply_final=apply_final),
        grid=(size // tm,),
        in_specs=in_specs,
        out_specs=pl.BlockSpec((tm, D), lambda i: (b0 + i, 0)),
        out_shape=jax.ShapeDtypeStruct((T, D), F32),
        input_output_aliases=aliases,
        compiler_params=_params(("parallel",)),
        name="ple_final",
    )(*args)


def _qk_permutation():
    half = ATTN_HEAD_DIM // 2
    perm = np.zeros(N_ATTN_HEADS * LANES, np.int32)
    for h in range(N_ATTN_HEADS):
        for c in range(2):
            for d in range(ATTN_HEAD_DIM):
                perm[h * LANES + (d // half) * 2 * half + c * half + d % half] = h * LANES + c * ATTN_HEAD_DIM + d
    return perm


def _rope_tables(seq):
    half = ATTN_HEAD_DIM // 2
    inv_freq = 1.0 / (ROPE_THETA ** (jnp.arange(half, dtype=F32) * 2.0 / ATTN_HEAD_DIM))
    ang = jnp.arange(seq).astype(F32)[:, None] * inv_freq[None, :]
    cos, sin = jnp.cos(ang), jnp.sin(ang)
    return jnp.tile(cos, (1, 4)), jnp.concatenate([-sin, -sin, sin, sin], axis=1)


def _token_chunks(total, big):
    sizes = [big] * (total // big)
    if big % 1024 == 0:
        sizes[-1:] = [big // 2, big // 4, big // 4]
    return sizes


def kernel(x, p, attn_norm_g, w_in, lambda_q1, lambda_k1, lambda_q2, lambda_k2, subln_g, conv_w, conv_b,
           conv_ln_g, conv_ln_b, w_out, ffn_norm_g, peer_wq, peer_keys, peer_u, peer_v, ple_norm_g,
           ple_w_gate, ple_w_proj, final_norm_g):
    B, S, D = x.shape
    T = B * S
    depth = w_in.shape[0]
    assert depth >= 1
    tm = min(512, S)
    tq = min(512, S)
    tr = min(256, S)
    tb = min(256, T)
    tc = min(4096, T)
    qk_half = N_ATTN_HEADS * 2 * ATTN_HEAD_DIM

    perm = _qk_permutation()
    col_order = np.concatenate([perm, qk_half + perm, np.arange(2 * qk_half, w_in.shape[2])])
    cos_t, sin_t = _rope_tables(S)
    row = lambda a: a.reshape(1, -1).astype(F32)

    h = x.reshape(T, D)
    for l in range(depth):
        lambda_init = 0.8 - 0.6 * math.exp(-0.3 * l)
        w_perm = w_in[l][:, col_order].astype(BF16)
        qk, v, cvg = _in_proj(h, row(attn_norm_g[l]), w_perm, cos_t, sin_t, S, tm)
        lamv = jnp.stack([lambda_q1[l], lambda_k1[l], lambda_q2[l], lambda_k2[l]]).astype(F32)
        attn = _diff_attn(qk, v, lamv, row(subln_g[l]), B, S, tq, lambda_init)
        conv = _conv_module(cvg, conv_w[l], row(conv_b[l]), row(conv_ln_g[l]), row(conv_ln_b[l]), S, tm)
        h1, m, experts, gates = _route(h, attn, conv, w_out[l].astype(BF16), row(ffn_norm_g[l]),
                                       peer_wq[l].astype(BF16), peer_keys[l].astype(BF16), tr)
        u_rows, v_words = _table_rows(peer_u[l]), _table_words(peer_v[l])
        ple_args = (p[l].reshape(T, -1), ple_w_proj[l].astype(BF16), row(ple_norm_g[l]), ple_w_gate[l].astype(BF16),
                    row(final_norm_g), l == depth - 1, tm)
        sizes = _token_chunks(T, tc)
        assert sum(sizes) == T and all(size % tm == 0 and size % tb == 0 for size in sizes), (T, sizes)
        ys, start = [], 0
        for size in sizes:
            hw = _peer_a(experts, m, gates, u_rows, tb, start, size)
            ys.append((start, _peer_y_sc(experts, hw, v_words, start, size)))
            start += size
        h = None
        for start, y_chunk in ys:
            h = _ple(h1, y_chunk, *ple_args, start, h)
    return h.reshape(B, S, D)
```
